```python
import jax, jax.numpy as jnp
from jax import lax
import numpy as np

D_MODEL = 1024
BATCH = 8
SEQ = 8192
DEPTH = 2
DEC_BATCH = 16
DEC_SEQ = 64
PAST_LEN = 4096

CHUNK = 64
N_MIXERS = 2
N_RET = (DEPTH + 1) // 2
N_SB = DEPTH // 2
RET_HEADS = 4
RET_DK = D_MODEL // RET_HEADS
RET_DV = 2 * RET_DK
RET_QK = RET_HEADS * RET_DK
RET_VW = RET_HEADS * RET_DV
ROPE_BASE = 10000.0
SB_HEADS = 16
SB_HD = D_MODEL // SB_HEADS
SB_QBLOCK = 128
PEER_HEADS = 8
PEER_QDIM = 256
N_KEYS = 128
N_EXPERTS = N_KEYS * N_KEYS
PEER_TOPK = 16
PEER_TOKEN_BLOCK = 256
LN_EPS = 1e-5
GN_EPS = 1e-6
ALPHA = (2 * DEPTH) ** 0.25
BETA_INIT = (8 * DEPTH) ** -0.25

kernel_name = 'retnet_stickbreak_peer_stream_step'


def layer_norm(x, g, b):
    xf = x.astype(jnp.float32)
    mu = jnp.mean(xf, axis=-1, keepdims=True)
    var = jnp.mean(jnp.square(xf - mu), axis=-1, keepdims=True)
    y = (xf - mu) * lax.rsqrt(var + LN_EPS) * g.astype(jnp.float32) + b.astype(jnp.float32)
    return y.astype(x.dtype)


def rotary(x, pos):
    half = x.shape[-1] // 2
    inv = ROPE_BASE ** (-jnp.arange(half, dtype=jnp.float32) / half)
    ang = pos.astype(jnp.float32)[:, None] * inv[None, :]
    cos, sin = jnp.cos(ang), jnp.sin(ang)
    xf = x.astype(jnp.float32)
    x1, x2 = xf[..., :half], xf[..., half:]
    return jnp.concatenate([x1 * cos - x2 * sin, x1 * sin + x2 * cos], axis=-1).astype(x.dtype)


def retention_log_decay():
    return jnp.log1p(-jnp.exp2(-5.0 - jnp.arange(RET_HEADS, dtype=jnp.float32)))


def retention_chunk(S, q, k, v, log_g):
    L = q.shape[2]
    pos = jnp.arange(L, dtype=jnp.float32)
    diff = pos[:, None] - pos[None, :]
    causal = diff >= 0
    dmask = jnp.where(causal, jnp.exp(log_g[:, None, None] * jnp.where(causal, diff, 0.0)), 0.0).astype(q.dtype)
    inner = jnp.einsum('bhnm,bhme->bhne', jnp.einsum('bhnd,bhmd->bhnm', q, k) * dmask, v)
    q_dec = jnp.exp(log_g[:, None] * (pos + 1.0)[None, :]).astype(q.dtype)
    cross = jnp.einsum('bhnd,bhde->bhne', q * q_dec[None, :, :, None], S)
    k_dec = jnp.exp(log_g[:, None] * (L - 1.0 - pos)[None, :]).astype(q.dtype)
    s_dec = jnp.exp(log_g * L).astype(S.dtype)
    S_new = S * s_dec[None, :, None, None] + jnp.einsum('bhmd,bhme->bhde', k * k_dec[None, :, :, None], v)
    return S_new.astype(S.dtype), inner + cross


def retention_mixer(x, S0, pos0, w_in, w_o):
    B, L, _ = x.shape
    q, k, v, g = jnp.split(x @ w_in, [RET_QK, 2 * RET_QK, 2 * RET_QK + RET_VW], axis=-1)
    pos = pos0 + jnp.arange(L)
    q = rotary(q.reshape(B, L, RET_HEADS, RET_DK).transpose(0, 2, 1, 3), pos)
    k = rotary(k.reshape(B, L, RET_HEADS, RET_DK).transpose(0, 2, 1, 3), pos) * (RET_DK ** -0.5)
    v = v.reshape(B, L, RET_HEADS, RET_DV).transpose(0, 2, 1, 3)
    log_g = retention_log_decay()
    if L <= CHUNK:
        S, o = retention_chunk(S0, q, k, v, log_g)
    else:
        nc = L // CHUNK
        def to_chunks(t):
            return jnp.moveaxis(t.reshape(B, RET_HEADS, nc, CHUNK, t.shape[-1]), 2, 0)
        S, o = lax.scan(lambda s, c: retention_chunk(s, c[0], c[1], c[2], log_g), S0,
                        (to_chunks(q), to_chunks(k), to_chunks(v)))
        o = jnp.moveaxis(o, 0, 2).reshape(B, RET_HEADS, L, RET_DV)
    of = o.astype(jnp.float32)
    mu = jnp.mean(of, axis=-1, keepdims=True)
    var = jnp.mean(jnp.square(of - mu), axis=-1, keepdims=True)
    y = ((of - mu) * lax.rsqrt(var + GN_EPS)).transpose(0, 2, 1, 3).reshape(B, L, RET_VW).astype(x.dtype)
    return (jax.nn.silu(g) * y) @ w_o, S


def stick_breaking_block(q, k, v, q_pos, k_pos):
    z = jnp.einsum('bhqd,bhkd->bhqk', q, k).astype(jnp.float32) * (SB_HD ** -0.5)
    mask = k_pos[None, :] < q_pos[:, None]
    log_remain = jnp.where(mask, jax.nn.log_sigmoid(-z), 0.0)
    log_pass = lax.cumsum(log_remain, axis=3, reverse=True) - log_remain
    w = jnp.where(mask, jnp.exp(jax.nn.log_sigmoid(z) + log_pass), 0.0)
    return jnp.einsum('bhqk,bhkd->bhqd', w.astype(v.dtype), v)


def stick_breaking_mixer(x, k_past, v_past, pos0, w_qkv, w_o):
    B, L, _ = x.shape
    q, k, v = jnp.split(x @ w_qkv, 3, axis=-1)
    q = q.reshape(B, L, SB_HEADS, SB_HD).transpose(0, 2, 1, 3)
    k = k.reshape(B, L, SB_HEADS, SB_HD).transpose(0, 2, 1, 3)
    v = v.reshape(B, L, SB_HEADS, SB_HD).transpose(0, 2, 1, 3)
    if k_past is None:
        k_all, v_all = k, v
    else:
        k_all = jnp.concatenate([k_past, k], axis=2)
        v_all = jnp.concatenate([v_past, v], axis=2)
    q_pos = pos0 + jnp.arange(L)
    k_pos = jnp.arange(k_all.shape[2])
    if L <= SB_QBLOCK:
        o = stick_breaking_block(q, k_all, v_all, q_pos, k_pos)
    else:
        nb = L // SB_QBLOCK
        qb = jnp.moveaxis(q.reshape(B, SB_HEADS, nb, SB_QBLOCK, SB_HD), 2, 0)
        pb = q_pos.reshape(nb, SB_QBLOCK)
        o = lax.map(lambda a: stick_breaking_block(a[0], k_all, v_all, a[1], k_pos), (qb, pb))
        o = jnp.moveaxis(o, 0, 2).reshape(B, SB_HEADS, L, SB_HD)
    out = o.transpose(0, 2, 1, 3).reshape(B, L, D_MODEL) @ w_o
    return out, k, v


def peer_tokens(xt, w_q, keys_a, keys_b, u, v):
    T = xt.shape[0]
    q = (xt @ w_q).reshape(T, PEER_HEADS, PEER_QDIM)
    qa, qb = jnp.split(q, 2, axis=-1)
    sa = jnp.einsum('thd,hkd->thk', qa, keys_a)
    sb = jnp.einsum('thd,hkd->thk', qb, keys_b)
    va, ia = lax.top_k(sa, PEER_TOPK)
    vb, ib = lax.top_k(sb, PEER_TOPK)
    cand_s = (va[..., :, None] + vb[..., None, :]).reshape(T, PEER_HEADS, PEER_TOPK * PEER_TOPK)
    cand_i = (ia[..., :, None] * N_KEYS + ib[..., None, :]).reshape(T, PEER_HEADS, PEER_TOPK * PEER_TOPK)
    s, j = lax.top_k(cand_s, PEER_TOPK)
    idx = jnp.take_along_axis(cand_i, j, axis=-1)
    gate = jax.nn.softmax(s.astype(jnp.float32), axis=-1).astype(xt.dtype)
    h = jax.nn.gelu(jnp.einsum('td,thkd->thk', xt, u[idx]), approximate=False)
    return jnp.einsum('thk,thkd->td', gate * h, v[idx])


def peer_ffn(x, w_q, keys_a, keys_b, u, v):
    B, L, D = x.shape
    T = B * L
    xt = x.reshape(T, D)
    if T <= PEER_TOKEN_BLOCK:
        out = peer_tokens(xt, w_q, keys_a, keys_b, u, v)
    else:
        nb = -(-T // PEER_TOKEN_BLOCK)
        pad = nb * PEER_TOKEN_BLOCK - T
        xb = jnp.pad(xt, ((0, pad), (0, 0))).reshape(nb, PEER_TOKEN_BLOCK, D)
        out = lax.map(lambda a: peer_tokens(a, w_q, keys_a, keys_b, u, v), xb)
        out = out.reshape(nb * PEER_TOKEN_BLOCK, D)[:T]
    return out.reshape(B, L, D)


def setup_inputs(seed: int = 0) -> dict:
    key = jax.random.key(seed)
    ks = jax.random.split(key, 16)
    f32 = jnp.float32

    def nrm(k, shape, scale):
        return jax.random.normal(k, shape, f32) * scale

    d_in = D_MODEL ** -0.5
    x_prompt = nrm(ks[0], (BATCH, SEQ, D_MODEL), 1.0)
    x_sample = nrm(ks[1], (DEC_BATCH, DEC_SEQ, D_MODEL), 1.0)
    state_ret = nrm(ks[2], (N_RET, DEC_BATCH, RET_HEADS, RET_DK, RET_DV), 0.5)
    cache_k = nrm(ks[3], (N_SB, DEC_BATCH, SB_HEADS, PAST_LEN, SB_HD), 1.0)
    cache_v = nrm(ks[4], (N_SB, DEC_BATCH, SB_HEADS, PAST_LEN, SB_HD), 1.0)
    ret_cols = jnp.concatenate([jnp.full((2 * RET_QK,), d_in, f32),
                                jnp.full((RET_VW,), d_in * BETA_INIT, f32),
                                jnp.full((RET_VW,), d_in, f32)])
    ret_w_in = nrm(ks[5], (N_RET, D_MODEL, 2 * RET_QK + 2 * RET_VW), 1.0) * ret_cols
    ret_w_o = nrm(ks[6], (N_RET, RET_VW, D_MODEL), RET_VW ** -0.5 * BETA_INIT)
    sb_cols = jnp.concatenate([jnp.full((2 * D_MODEL,), d_in, f32),
                               jnp.full((D_MODEL,), d_in * BETA_INIT, f32)])
    sb_w_qkv = nrm(ks[7], (N_SB, D_MODEL, 3 * D_MODEL), 1.0) * sb_cols
    sb_w_o = nrm(ks[8], (N_SB, D_MODEL, D_MODEL), d_in * BETA_INIT)
    peer_w_q = nrm(ks[9], (DEPTH, D_MODEL, PEER_HEADS * PEER_QDIM), d_in)
    peer_keys_a = nrm(ks[10], (DEPTH, PEER_HEADS, N_KEYS, PEER_QDIM // 2), (PEER_QDIM // 2) ** -0.5)
    peer_keys_b = nrm(ks[11], (DEPTH, PEER_HEADS, N_KEYS, PEER_QDIM // 2), (PEER_QDIM // 2) ** -0.5)
    peer_u = nrm(ks[12], (DEPTH, N_EXPERTS, D_MODEL), d_in)
    peer_v = nrm(ks[13], (DEPTH, N_EXPERTS, D_MODEL), BETA_INIT * PEER_HEADS ** -0.5)
    ln_g = 1.0 + nrm(ks[14], (DEPTH, 2, D_MODEL), 0.02)
    ln_b = nrm(ks[15], (DEPTH, 2, D_MODEL), 0.02)
    return {'x_prompt': x_prompt, 'x_sample': x_sample, 'state_ret': state_ret,
            'cache_k': cache_k, 'cache_v': cache_v,
            'ret_w_in': ret_w_in, 'ret_w_o': ret_w_o, 'sb_w_qkv': sb_w_qkv, 'sb_w_o': sb_w_o,
            'peer_w_q': peer_w_q, 'peer_keys_a': peer_keys_a, 'peer_keys_b': peer_keys_b,
            'peer_u': peer_u, 'peer_v': peer_v, 'ln_g': ln_g, 'ln_b': ln_b}


def reference(x_prompt, x_sample, state_ret, cache_k, cache_v, ret_w_in, ret_w_o, sb_w_qkv, sb_w_o,
              peer_w_q, peer_keys_a, peer_keys_b, peer_u, peer_v, ln_g, ln_b):
    xp, xs = x_prompt, x_sample
    ret_p, ret_s, kp, vp, ksm, vsm = [], [], [], [], [], []
    for i in range(DEPTH):
        j = i // N_MIXERS
        if i % N_MIXERS == 0:
            s0 = jnp.zeros((xp.shape[0], RET_HEADS, RET_DK, RET_DV), xp.dtype)
            mp, sp = retention_mixer(xp, s0, 0, ret_w_in[j], ret_w_o[j])
            ms, ss = retention_mixer(xs, state_ret[j], PAST_LEN, ret_w_in[j], ret_w_o[j])
            ret_p.append(sp)
            ret_s.append(ss)
        else:
            mp, k_new_p, v_new_p = stick_breaking_mixer(xp, None, None, 0, sb_w_qkv[j], sb_w_o[j])
            ms, k_new_s, v_new_s = stick_breaking_mixer(xs, cache_k[j], cache_v[j], PAST_LEN, sb_w_qkv[j], sb_w_o[j])
            kp.append(k_new_p)
            vp.append(v_new_p)
            ksm.append(k_new_s)
            vsm.append(v_new_s)
        xp = layer_norm(ALPHA * xp + mp, ln_g[i, 0], ln_b[i, 0])
        xs = layer_norm(ALPHA * xs + ms, ln_g[i, 0], ln_b[i, 0])
        xp = layer_norm(ALPHA * xp + peer_ffn(xp, peer_w_q[i], peer_keys_a[i], peer_keys_b[i], peer_u[i], peer_v[i]),
                        ln_g[i, 1], ln_b[i, 1])
        xs = layer_norm(ALPHA * xs + peer_ffn(xs, peer_w_q[i], peer_keys_a[i], peer_keys_b[i], peer_u[i], peer_v[i]),
                        ln_g[i, 1], ln_b[i, 1])
    return (xp, xs, jnp.stack(ret_p), jnp.stack(ret_s), jnp.stack(kp), jnp.stack(vp), jnp.stack(ksm), jnp.stack(vsm))
```

```python
import functools

import jax
import jax.numpy as jnp
from jax import lax
from jax.experimental import pallas as pl
from jax.experimental.pallas import tpu as pltpu

D_MODEL = 1024
DEPTH = 2
PAST_LEN = 4096
RET_HEADS = 4
RET_DK = D_MODEL // RET_HEADS
RET_DV = 2 * RET_DK
RET_QK = RET_HEADS * RET_DK
RET_VW = RET_HEADS * RET_DV
ROPE_BASE = 10000.0
SB_HEADS = 16
SB_HD = D_MODEL // SB_HEADS
PEER_HEADS = 8
PEER_QDIM = 256
N_KEYS = 128
N_EXPERTS = N_KEYS * N_KEYS
PEER_TOPK = 16
LN_EPS = 1e-5
GN_EPS = 1e-6
ALPHA = (2 * DEPTH) ** 0.25

LANES = 128
VMEM_LIMIT = 48 * 1024 * 1024
NEG = -1e30
LOG_F32_UNDERFLOW = 104.0

BF16 = jnp.bfloat16
F32 = jnp.float32


def _cparams(*sem):
    return pltpu.CompilerParams(dimension_semantics=sem, vmem_limit_bytes=VMEM_LIMIT)


def _mm_kernel(x_ref, w_ref, o_ref):
    o_ref[...] = jnp.dot(x_ref[...].astype(BF16), w_ref[...],
                         preferred_element_type=F32).astype(o_ref.dtype)


def _mm(x, w, out_dtype, tm=512, tn=1024):
    M, K = x.shape
    N = w.shape[1]
    tm = min(tm, M)
    tn = min(tn, N)
    assert M % tm == 0 and N % tn == 0
    return pl.pallas_call(
        _mm_kernel,
        grid=(M // tm, N // tn),
        in_specs=[pl.BlockSpec((tm, K), lambda i, j: (i, 0)),
                  pl.BlockSpec((K, tn), lambda i, j: (0, j))],
        out_specs=pl.BlockSpec((tm, tn), lambda i, j: (i, j)),
        out_shape=jax.ShapeDtypeStruct((M, N), out_dtype),
        compiler_params=_cparams("parallel", "arbitrary"),
        name="proj_matmul",
    )(x, w)


def _layer_norm_rows(r, g, b):
    mu = jnp.mean(r, axis=-1, keepdims=True)
    d = r - mu
    var = jnp.mean(d * d, axis=-1, keepdims=True)
    return d * lax.rsqrt(var + LN_EPS) * g + b


def _mm_res_ln_kernel(a_ref, w_ref, x_ref, g_ref, b_ref, o_ref):
    m = jnp.dot(a_ref[...].astype(BF16), w_ref[...], preferred_element_type=F32)
    o_ref[...] = _layer_norm_rows(ALPHA * x_ref[...] + m, g_ref[...], b_ref[...])


def _mm_res_ln(a, w, x, g, b, tm=512):
    M, K = a.shape
    D = w.shape[1]
    tm = min(tm, M)
    assert M % tm == 0
    return pl.pallas_call(
        _mm_res_ln_kernel,
        grid=(M // tm,),
        in_specs=[pl.BlockSpec((tm, K), lambda i: (i, 0)),
                  pl.BlockSpec((K, D), lambda i: (0, 0)),
                  pl.BlockSpec((tm, D), lambda i: (i, 0)),
                  pl.BlockSpec((1, D), lambda i: (0, 0)),
                  pl.BlockSpec((1, D), lambda i: (0, 0))],
        out_specs=pl.BlockSpec((tm, D), lambda i: (i, 0)),
        out_shape=jax.ShapeDtypeStruct((M, D), F32),
        compiler_params=_cparams("parallel"),
        name="out_proj_residual_ln",
    )(a, w, x, g.reshape(1, D), b.reshape(1, D))


def _res_ln_kernel(x_ref, m_ref, g_ref, b_ref, o_ref):
    o_ref[...] = _layer_norm_rows(ALPHA * x_ref[...] + m_ref[...], g_ref[...], b_ref[...])


def _res_ln(x, m, g, b, tm=512):
    M, D = x.shape
    tm = min(tm, M)
    assert M % tm == 0
    return pl.pallas_call(
        _res_ln_kernel,
        grid=(M // tm,),
        in_specs=[pl.BlockSpec((tm, D), lambda i: (i, 0)),
                  pl.BlockSpec((tm, D), lambda i: (i, 0)),
                  pl.BlockSpec((1, D), lambda i: (0, 0)),
                  pl.BlockSpec((1, D), lambda i: (0, 0))],
        out_specs=pl.BlockSpec((tm, D), lambda i: (i, 0)),
        out_shape=jax.ShapeDtypeStruct((M, D), F32),
        compiler_params=_cparams("parallel"),
        name="residual_ln",
    )(x, m, g.reshape(1, D), b.reshape(1, D))


def _retention_kernel(q_ref, k_ref, v_ref, g_ref, cos_ref, sin_ref, dm_ref, qd_ref, kd_ref,
                      sd_ref, s0_ref, y_ref, sout_ref, s_scr):
    c = pl.program_id(2)

    @pl.when(c == 0)
    def _():
        s_scr[...] = s0_ref[0, 0]

    cos = cos_ref[...]
    sin = sin_ref[...]
    half = RET_DK // 2

    def rot(x):
        x1 = x[:, :half]
        x2 = x[:, half:]
        return jnp.concatenate([x1 * cos - x2 * sin, x1 * sin + x2 * cos], axis=-1)

    qr = rot(q_ref[...])
    kr = rot(k_ref[...]) * (RET_DK ** -0.5)
    v = v_ref[...].astype(BF16)
    S = s_scr[...]

    scores = lax.dot_general(qr.astype(BF16), kr.astype(BF16), (((1,), (1,)), ((), ())),
                             preferred_element_type=F32)
    inner = jnp.dot((scores * dm_ref[0]).astype(BF16), v, preferred_element_type=F32)
    cross = jnp.dot((qr * qd_ref[0]).astype(BF16), S.astype(BF16), preferred_element_type=F32)
    o = inner + cross

    kdec = (kr * kd_ref[0]).astype(BF16)
    s_new = S * sd_ref[0] + lax.dot_general(kdec, v, (((0,), (0,)), ((), ())),
                                            preferred_element_type=F32)
    s_scr[...] = s_new

    mu = jnp.mean(o, axis=-1, keepdims=True)
    d = o - mu
    var = jnp.mean(d * d, axis=-1, keepdims=True)
    yn = d * lax.rsqrt(var + GN_EPS)
    g = g_ref[...]
    y_ref[...] = (g * jax.nn.sigmoid(g) * yn).astype(y_ref.dtype)

    @pl.when(c == pl.num_programs(2) - 1)
    def _():
        sout_ref[0, 0] = s_new


def _retention(qkvg, s0, cos, sin, batch, seqlen, pos0, chunk):
    C = chunk
    nc = seqlen // C
    assert seqlen % C == 0 and pos0 % C == 0
    pblk = pos0 // C
    pos = jnp.arange(C, dtype=F32)
    diff = pos[:, None] - pos[None, :]
    causal = diff >= 0
    log_g = jnp.log1p(-jnp.exp2(-5.0 - jnp.arange(RET_HEADS, dtype=F32)))
    dmask = jnp.where(causal, jnp.exp(log_g[:, None, None] * jnp.where(causal, diff, 0.0)), 0.0)
    q_dec = jnp.exp(log_g[:, None] * (pos + 1.0)[None, :])
    k_dec = jnp.exp(log_g[:, None] * (C - 1.0 - pos)[None, :])
    s_dec = jnp.exp(log_g * C)
    qd = jnp.broadcast_to(q_dec[:, :, None], (RET_HEADS, C, RET_DK))
    kd = jnp.broadcast_to(k_dec[:, :, None], (RET_HEADS, C, RET_DK))
    sd = jnp.broadcast_to(s_dec[:, None, None], (RET_HEADS, 1, RET_DV))

    nq = RET_QK // RET_DK
    nv = 2 * RET_QK // RET_DV
    ng = nv + RET_VW // RET_DV
    row = lambda b, h, c: b * nc + c
    y, s_out = pl.pallas_call(
        _retention_kernel,
        grid=(batch, RET_HEADS, nc),
        in_specs=[
            pl.BlockSpec((C, RET_DK), lambda b, h, c: (row(b, h, c), h)),
            pl.BlockSpec((C, RET_DK), lambda b, h, c: (row(b, h, c), nq + h)),
            pl.BlockSpec((C, RET_DV), lambda b, h, c: (row(b, h, c), nv + h)),
            pl.BlockSpec((C, RET_DV), lambda b, h, c: (row(b, h, c), ng + h)),
            pl.BlockSpec((C, RET_DK // 2), lambda b, h, c: (pblk + c, 0)),
            pl.BlockSpec((C, RET_DK // 2), lambda b, h, c: (pblk + c, 0)),
            pl.BlockSpec((1, C, C), lambda b, h, c: (h, 0, 0)),
            pl.BlockSpec((1, C, RET_DK), lambda b, h, c: (h, 0, 0)),
            pl.BlockSpec((1, C, RET_DK), lambda b, h, c: (h, 0, 0)),
            pl.BlockSpec((1, 1, RET_DV), lambda b, h, c: (h, 0, 0)),
            pl.BlockSpec((1, 1, RET_DK, RET_DV), lambda b, h, c: (b, h, 0, 0)),
        ],
        out_specs=[
            pl.BlockSpec((C, RET_DV), lambda b, h, c: (row(b, h, c), h)),
            pl.BlockSpec((1, 1, RET_DK, RET_DV), lambda b, h, c: (b, h, 0, 0)),
        ],
        out_shape=[jax.ShapeDtypeStruct((batch * seqlen, RET_VW), BF16),
                   jax.ShapeDtypeStruct((batch, RET_HEADS, RET_DK, RET_DV), F32)],
        scratch_shapes=[pltpu.VMEM((RET_DK, RET_DV), F32)],
        compiler_params=_cparams("parallel", "parallel", "arbitrary"),
        name="retention",
    )(qkvg, qkvg, qkvg, qkvg, cos, sin, dmask, qd, kd, sd, s0)
    return y, s_out


def _log_sigmoid(z):
    return jnp.minimum(z, 0.0) - jnp.log1p(jnp.exp(-jnp.abs(z)))


def _split3(x):
    hi = x.astype(BF16)
    r1 = x - hi.astype(F32)
    mid = r1.astype(BF16)
    lo = (r1 - mid.astype(F32)).astype(BF16)
    return hi, mid, lo


def _stickbreak_kernel(q_ref, k_ref, v_ref, o_ref, *, pos0, qb, kb):
    qi = pl.program_id(1)
    q = q_ref[0].astype(BF16)
    q_first = pos0 + qi * qb
    q_pos = q_first + lax.broadcasted_iota(jnp.int32, (qb, 1), 0)
    tri = (lax.broadcasted_iota(jnp.int32, (kb, kb), 0)
           > lax.broadcasted_iota(jnp.int32, (kb, kb), 1)).astype(BF16)
    start_blk = (q_first + qb - 2) // kb

    def cond(st):
        blk, carry, _ = st
        return jnp.logical_and(blk >= 0, jnp.max(carry) > -LOG_F32_UNDERFLOW)

    def body(st):
        blk, carry, acc = st
        off = pl.multiple_of(blk * kb, kb)
        ks = k_ref[0, pl.ds(off, kb), :].astype(BF16)
        vs = v_ref[0, pl.ds(off, kb), :].astype(BF16)
        z = lax.dot_general(q, ks, (((1,), (1,)), ((), ())),
                            preferred_element_type=F32) * (SB_HD ** -0.5)
        k_pos = off + lax.broadcasted_iota(jnp.int32, (1, kb), 1)
        mask = k_pos < q_pos
        ls = _log_sigmoid(z)
        lr = jnp.where(mask, ls - z, 0.0)
        hi, mid, lo = _split3(lr)
        suffix = (jnp.dot(hi, tri, preferred_element_type=F32)
                  + jnp.dot(mid, tri, preferred_element_type=F32)
                  + jnp.dot(lo, tri, preferred_element_type=F32))
        w = jnp.where(mask, jnp.exp(ls + (carry + suffix)), 0.0)
        acc = acc + jnp.dot(w.astype(BF16), vs, preferred_element_type=F32)
        carry = carry + jnp.sum(lr, axis=-1, keepdims=True)
        return blk - 1, carry, acc

    init = (start_blk, jnp.zeros((qb, 1), F32), jnp.zeros((qb, SB_HD), F32))
    _, _, acc = lax.while_loop(cond, body, init)
    o_ref[0] = acc


def _stickbreak(q, k_all, v_all, pos0, qb, kb):
    BH, Lq, hd = q.shape
    Lk = k_all.shape[1]
    assert Lq % qb == 0 and Lk % kb == 0 and pos0 + Lq <= Lk
    return pl.pallas_call(
        functools.partial(_stickbreak_kernel, pos0=pos0, qb=qb, kb=kb),
        grid=(BH, Lq // qb),
        in_specs=[pl.BlockSpec((1, qb, hd), lambda b, i: (b, i, 0)),
                  pl.BlockSpec((1, Lk, hd), lambda b, i: (b, 0, 0)),
                  pl.BlockSpec((1, Lk, hd), lambda b, i: (b, 0, 0))],
        out_specs=pl.BlockSpec((1, qb, hd), lambda b, i: (b, i, 0)),
        out_shape=jax.ShapeDtypeStruct((BH, Lq, hd), F32),
        compiler_params=_cparams("parallel", "arbitrary"),
        name="stickbreak_attention",
    )(q, k_all, v_all)


def _top_values(s, n):
    m = jnp.max(s, axis=0)
    vals = [m]
    for _ in range(n - 1):
        m = jnp.max(jnp.where(s < m, s, NEG), axis=0)
        vals.append(m)
    return vals


def _peer_route_kernel(xT_ref, wqT_ref, ka_kh_ref, kb_kh_ref, kb_hk_ref, sa_ref, sb_ref, thr_ref,
                       sbk_scr):
    nq = PEER_HEADS * PEER_QDIM // 2
    tt = xT_ref.shape[1]
    q = jnp.dot(wqT_ref[...], xT_ref[...], preferred_element_type=F32).astype(BF16)
    qa = q[:nq]
    qb = q[nq:]
    sa_ref[...] = jnp.dot(ka_kh_ref[...], qa, preferred_element_type=F32)
    sb_ref[...] = jnp.dot(kb_hk_ref[...], qb, preferred_element_type=F32)
    sbk_scr[...] = jnp.dot(kb_kh_ref[...], qb, preferred_element_type=F32)
    for g in range(tt // LANES):
        lanes = slice(g * LANES, (g + 1) * LANES)
        va = _top_values(sa_ref[:, lanes].reshape(N_KEYS, PEER_HEADS, LANES), PEER_TOPK)
        vb = _top_values(sbk_scr[:, lanes].reshape(N_KEYS, PEER_HEADS, LANES), PEER_TOPK)
        cand = jnp.stack([va[i] + vb[j] for i in range(PEER_TOPK) for j in range(PEER_TOPK)
                          if (i + 1) * (j + 1) <= PEER_TOPK], axis=0)
        tau = _top_values(cand, PEER_TOPK)[-1]
        top = va[0] + vb[0]
        zsum = jnp.sum(jnp.where(cand >= tau, jnp.exp(cand - top), 0.0), axis=0)
        thr_ref[0:PEER_HEADS, lanes] = tau
        thr_ref[PEER_HEADS:2 * PEER_HEADS, lanes] = top + jnp.log(zsum)


def _peer_route(xT, wqT, ka_kh, kb_kh, kb_hk, tt):
    D, T = xT.shape
    assert T % tt == 0
    rows = PEER_HEADS * N_KEYS
    nq = PEER_HEADS * PEER_QDIM // 2
    return pl.pallas_call(
        _peer_route_kernel,
        grid=(T // tt,),
        in_specs=[pl.BlockSpec((D, tt), lambda i: (0, i)),
                  pl.BlockSpec((2 * nq, D), lambda i: (0, 0)),
                  pl.BlockSpec((rows, nq), lambda i: (0, 0)),
                  pl.BlockSpec((rows, nq), lambda i: (0, 0)),
                  pl.BlockSpec((rows, nq), lambda i: (0, 0))],
        out_specs=[pl.BlockSpec((rows, tt), lambda i: (0, i)),
                   pl.BlockSpec((rows, tt), lambda i: (0, i)),
                   pl.BlockSpec((2 * PEER_HEADS, tt), lambda i: (0, i))],
        out_shape=[jax.ShapeDtypeStruct((rows, T), F32),
                   jax.ShapeDtypeStruct((rows, T), F32),
                   jax.ShapeDtypeStruct((2 * PEER_HEADS, T), F32)],
        scratch_shapes=[pltpu.VMEM((rows, tt), F32)],
        compiler_params=_cparams("parallel"),
        name="peer_route",
    )(xT, wqT, ka_kh, kb_kh, kb_hk)


def _peer_weights(w_q, keys_a, keys_b, u, v):
    half = PEER_QDIM // 2
    wqT = w_q.reshape(D_MODEL, PEER_HEADS, 2, half).transpose(2, 1, 3, 0)
    wqT = wqT.reshape(PEER_HEADS * PEER_QDIM, D_MODEL)
    eye = jnp.eye(PEER_HEADS, dtype=keys_a.dtype)
    n = PEER_HEADS * N_KEYS
    ka_kh = jnp.einsum('hkd,hg->khgd', keys_a, eye).reshape(n, PEER_HEADS * half)
    kb_kh = jnp.einsum('hkd,hg->khgd', keys_b, eye).reshape(n, PEER_HEADS * half)
    kb_hk = jnp.einsum('hkd,hg->hkgd', keys_b, eye).reshape(n, PEER_HEADS * half)
    return tuple(t.astype(BF16) for t in (wqT, ka_kh, kb_kh, kb_hk, u, v.T))


def _gelu(x):
    return 0.5 * x * (1.0 + lax.erf(x * (2.0 ** -0.5)))


def _peer_dense_kernel(xT_ref, u_ref, vT_ref, sa_ref, sb_ref, thr_ref, o_ref, acc_ref, h_scr, w_scr):
    j = pl.program_id(1)
    ec, tt = h_scr.shape
    na = ec // N_KEYS

    @pl.when(j == 0)
    def _():
        acc_ref[...] = jnp.zeros_like(acc_ref)

    h_scr[...] = jnp.dot(u_ref[...], xT_ref[...], preferred_element_type=F32)

    for g in range(tt // LANES):
        lanes = slice(g * LANES, (g + 1) * LANES)
        tau = [thr_ref[h:h + 1, lanes] for h in range(PEER_HEADS)]
        shift = [thr_ref[PEER_HEADS + h:PEER_HEADS + h + 1, lanes] for h in range(PEER_HEADS)]
        for al in range(na):
            a = j * na + al
            sa_heads = sa_ref[pl.ds(pl.multiple_of(a * PEER_HEADS, PEER_HEADS), PEER_HEADS), lanes]
            gate = jnp.zeros((N_KEYS, LANES), F32)
            for h in range(PEER_HEADS):
                sa_row = sa_heads[h:h + 1, :]
                s = sb_ref[h * N_KEYS:(h + 1) * N_KEYS, lanes] + sa_row
                gate = gate + jnp.where(s >= tau[h], jnp.exp(s - shift[h]), 0.0)
            rows = slice(al * N_KEYS, (al + 1) * N_KEYS)
            w_scr[rows, lanes] = (gate * _gelu(h_scr[rows, lanes])).astype(BF16)

    acc_ref[...] += jnp.dot(vT_ref[...], w_scr[...], preferred_element_type=F32)

    @pl.when(j == pl.num_programs(1) - 1)
    def _():
        o_ref[...] = acc_ref[...]


def _peer_dense(xT, u, vT, sa, sb, thr, tt, ec):
    D, T = xT.shape
    E = u.shape[0]
    assert T % tt == 0 and E % ec == 0 and ec % N_KEYS == 0
    rows = PEER_HEADS * N_KEYS
    return pl.pallas_call(
        _peer_dense_kernel,
        grid=(T // tt, E // ec),
        in_specs=[pl.BlockSpec((D, tt), lambda i, j: (0, i)),
                  pl.BlockSpec((ec, D), lambda i, j: (j, 0)),
                  pl.BlockSpec((D, ec), lambda i, j: (0, j)),
                  pl.BlockSpec((rows, tt), lambda i, j: (0, i)),
                  pl.BlockSpec((rows, tt), lambda i, j: (0, i)),
                  pl.BlockSpec((2 * PEER_HEADS, tt), lambda i, j: (0, i))],
        out_specs=pl.BlockSpec((D, tt), lambda i, j: (0, i)),
        out_shape=jax.ShapeDtypeStruct((D, T), F32),
        scratch_shapes=[pltpu.VMEM((D, tt), F32), pltpu.VMEM((ec, tt), F32), pltpu.VMEM((ec, tt), BF16)],
        compiler_params=_cparams("parallel", "arbitrary"),
        name="peer_dense",
    )(xT, u, vT, sa, sb, thr)


def _peer(x, weights, tt=512, ec=512):
    wqT, ka_kh, kb_kh, kb_hk, u, vT = weights
    tt = min(tt, x.shape[0])
    xT = x.T.astype(BF16)
    sa, sb, thr = _peer_route(xT, wqT, ka_kh, kb_kh, kb_hk, tt)
    return _peer_dense(xT, u, vT, sa, sb, thr, tt, ec).T


def _rope_tables(length):
    half = RET_DK // 2
    inv = ROPE_BASE ** (-jnp.arange(half, dtype=F32) / half)
    ang = jnp.arange(length, dtype=F32)[:, None] * inv[None, :]
    return jnp.cos(ang), jnp.sin(ang)


def _heads_first(t, batch, seqlen):
    return t.reshape(batch, seqlen, SB_HEADS, SB_HD).transpose(0, 2, 1, 3)


def kernel(x_prompt, x_sample, state_ret, cache_k, cache_v, ret_w_in, ret_w_o, sb_w_qkv, sb_w_o,
           peer_w_q, peer_keys_a, peer_keys_b, peer_u, peer_v, ln_g, ln_b):
    B, L, D = x_prompt.shape
    Bs, Ls, _ = x_sample.shape
    xp = x_prompt.reshape(B * L, D)
    xs = x_sample.reshape(Bs * Ls, D)
    cos, sin = _rope_tables(max(L, PAST_LEN + Ls))

    w_in = ret_w_in[0].astype(BF16)
    w_o = ret_w_o[0].astype(BF16)
    s0_p = jnp.zeros((B, RET_HEADS, RET_DK, RET_DV), F32)
    yp, ret_p = _retention(_mm(xp, w_in, F32), s0_p, cos, sin, B, L, 0, 256)
    ys, ret_s = _retention(_mm(xs, w_in, F32), state_ret[0], cos, sin, Bs, Ls, PAST_LEN, Ls)
    xp = _mm_res_ln(yp, w_o, xp, ln_g[0, 0], ln_b[0, 0])
    xs = _mm_res_ln(ys, w_o, xs, ln_g[0, 0], ln_b[0, 0])

    def peer_layer(i, xp, xs):
        weights = _peer_weights(peer_w_q[i], peer_keys_a[i], peer_keys_b[i], peer_u[i], peer_v[i])
        xp = _res_ln(xp, _peer(xp, weights), ln_g[i, 1], ln_b[i, 1])
        xs = _res_ln(xs, _peer(xs, weights), ln_g[i, 1], ln_b[i, 1])
        return xp, xs

    xp, xs = peer_layer(0, xp, xs)

    w_qkv = sb_w_qkv[0].astype(BF16)
    w_o = sb_w_o[0].astype(BF16)
    qp, kp, vp = [_heads_first(t, B, L) for t in jnp.split(_mm(xp, w_qkv, F32), 3, axis=-1)]
    qs, ks, vs = [_heads_first(t, Bs, Ls) for t in jnp.split(_mm(xs, w_qkv, F32), 3, axis=-1)]
    op = _stickbreak(qp.reshape(B * SB_HEADS, L, SB_HD), kp.reshape(B * SB_HEADS, L, SB_HD),
                     vp.reshape(B * SB_HEADS, L, SB_HD), 0, 128, 128)
    k_all = jnp.concatenate([cache_k[0], ks], axis=2).reshape(Bs * SB_HEADS, PAST_LEN + Ls, SB_HD)
    v_all = jnp.concatenate([cache_v[0], vs], axis=2).reshape(Bs * SB_HEADS, PAST_LEN + Ls, SB_HD)
    os_ = _stickbreak(qs.reshape(Bs * SB_HEADS, Ls, SB_HD), k_all, v_all, PAST_LEN, Ls, Ls)
    op = op.reshape(B, SB_HEADS, L, SB_HD).transpose(0, 2, 1, 3).reshape(B * L, D)
    os_ = os_.reshape(Bs, SB_HEADS, Ls, SB_HD).transpose(0, 2, 1, 3).reshape(Bs * Ls, D)
    xp = _mm_res_ln(op, w_o, xp, ln_g[1, 0], ln_b[1, 0])
    xs = _mm_res_ln(os_, w_o, xs, ln_g[1, 0], ln_b[1, 0])

    xp, xs = peer_layer(1, xp, xs)

    return (xp.reshape(B, L, D), xs.reshape(Bs, Ls, D), ret_p[None], ret_s[None],
            kp[None], vp[None], ks[None], vs[None])
```

```python
import functools

import jax
import jax.numpy as jnp
from jax import lax
from jax.experimental import pallas as pl
from jax.experimental.pallas import tpu as pltpu

D_MODEL = 1024
DEPTH = 2
PAST_LEN = 4096
RET_HEADS = 4
RET_DK = D_MODEL // RET_HEADS
RET_DV = 2 * RET_DK
RET_QK = RET_HEADS * RET_DK
RET_VW = RET_HEADS * RET_DV
ROPE_BASE = 10000.0
SB_HEADS = 16
SB_HD = D_MODEL // SB_HEADS
PEER_HEADS = 8
PEER_QDIM = 256
N_KEYS = 128
N_EXPERTS = N_KEYS * N_KEYS
PEER_TOPK = 16
LN_EPS = 1e-5
GN_EPS = 1e-6
ALPHA = (2 * DEPTH) ** 0.25

LANES = 128
MXU_TILE = 256
GATE_ROWS = 32
VMEM_LIMIT = 48 * 1024 * 1024
NEG = -1e30
LOG_F32_UNDERFLOW = 104.0

BF16 = jnp.bfloat16
F32 = jnp.float32


def _cparams(*sem):
    return pltpu.CompilerParams(dimension_semantics=sem, vmem_limit_bytes=VMEM_LIMIT)


def _mm_kernel(x_ref, w_ref, o_ref):
    o_ref[...] = jnp.dot(x_ref[...].astype(BF16), w_ref[...],
                         preferred_element_type=F32).astype(o_ref.dtype)


def _mm(x, w, out_dtype, tm=512, tn=1024):
    M, K = x.shape
    N = w.shape[1]
    tm = min(tm, M)
    tn = min(tn, N)
    assert M % tm == 0 and N % tn == 0
    return pl.pallas_call(
        _mm_kernel,
        grid=(M // tm, N // tn),
        in_specs=[pl.BlockSpec((tm, K), lambda i, j: (i, 0)),
                  pl.BlockSpec((K, tn), lambda i, j: (0, j))],
        out_specs=pl.BlockSpec((tm, tn), lambda i, j: (i, j)),
        out_shape=jax.ShapeDtypeStruct((M, N), out_dtype),
        compiler_params=_cparams("parallel", "arbitrary"),
        name="proj_matmul",
    )(x, w)


def _layer_norm_rows(r, g, b):
    mu = jnp.mean(r, axis=-1, keepdims=True)
    d = r - mu
    var = jnp.mean(d * d, axis=-1, keepdims=True)
    return d * lax.rsqrt(var + LN_EPS) * g + b


def _mm_res_ln_kernel(a_ref, w_ref, x_ref, g_ref, b_ref, o_ref):
    m = jnp.dot(a_ref[...].astype(BF16), w_ref[...], preferred_element_type=F32)
    o_ref[...] = _layer_norm_rows(ALPHA * x_ref[...] + m, g_ref[...], b_ref[...])


def _mm_res_ln(a, w, x, g, b, tm=512):
    M, K = a.shape
    D = w.shape[1]
    tm = min(tm, M)
    assert M % tm == 0
    return pl.pallas_call(
        _mm_res_ln_kernel,
        grid=(M // tm,),
        in_specs=[pl.BlockSpec((tm, K), lambda i: (i, 0)),
                  pl.BlockSpec((K, D), lambda i: (0, 0)),
                  pl.BlockSpec((tm, D), lambda i: (i, 0)),
                  pl.BlockSpec((1, D), lambda i: (0, 0)),
                  pl.BlockSpec((1, D), lambda i: (0, 0))],
        out_specs=pl.BlockSpec((tm, D), lambda i: (i, 0)),
        out_shape=jax.ShapeDtypeStruct((M, D), F32),
        compiler_params=_cparams("parallel"),
        name="out_proj_residual_ln",
    )(a, w, x, g.reshape(1, D), b.reshape(1, D))


def _res_ln_kernel(x_ref, m_ref, g_ref, b_ref, o_ref):
    o_ref[...] = _layer_norm_rows(ALPHA * x_ref[...] + m_ref[...], g_ref[...], b_ref[...])


def _res_ln(x, m, g, b, tm=512):
    M, D = x.shape
    tm = min(tm, M)
    assert M % tm == 0
    return pl.pallas_call(
        _res_ln_kernel,
        grid=(M // tm,),
        in_specs=[pl.BlockSpec((tm, D), lambda i: (i, 0)),
                  pl.BlockSpec((tm, D), lambda i: (i, 0)),
                  pl.BlockSpec((1, D), lambda i: (0, 0)),
                  pl.BlockSpec((1, D), lambda i: (0, 0))],
        out_specs=pl.BlockSpec((tm, D), lambda i: (i, 0)),
        out_shape=jax.ShapeDtypeStruct((M, D), F32),
        compiler_params=_cparams("parallel"),
        name="residual_ln",
    )(x, m, g.reshape(1, D), b.reshape(1, D))


def _retention_kernel(q_ref, k_ref, v_ref, g_ref, cos_ref, sin_ref, dm_ref, qd_ref, kd_ref,
                      sd_ref, s0_ref, y_ref, sout_ref, s_scr):
    c = pl.program_id(2)

    @pl.when(c == 0)
    def _():
        s_scr[...] = s0_ref[0, 0]

    cos = cos_ref[...]
    sin = sin_ref[...]
    half = RET_DK // 2

    def rot(x):
        x1 = x[:, :half]
        x2 = x[:, half:]
        return jnp.concatenate([x1 * cos - x2 * sin, x1 * sin + x2 * cos], axis=-1)

    qr = rot(q_ref[...])
    kr = rot(k_ref[...]) * (RET_DK ** -0.5)
    v = v_ref[...].astype(BF16)
    S = s_scr[...]

    scores = lax.dot_general(qr.astype(BF16), kr.astype(BF16), (((1,), (1,)), ((), ())),
                             preferred_element_type=F32)
    inner = jnp.dot((scores * dm_ref[0]).astype(BF16), v, preferred_element_type=F32)
    cross = jnp.dot((qr * qd_ref[0]).astype(BF16), S.astype(BF16), preferred_element_type=F32)
    o = inner + cross

    kdec = (kr * kd_ref[0]).astype(BF16)
    s_new = S * sd_ref[0] + lax.dot_general(kdec, v, (((0,), (0,)), ((), ())),
                                            preferred_element_type=F32)
    s_scr[...] = s_new

    mu = jnp.mean(o, axis=-1, keepdims=True)
    d = o - mu
    var = jnp.mean(d * d, axis=-1, keepdims=True)
    yn = d * lax.rsqrt(var + GN_EPS)
    g = g_ref[...]
    y_ref[...] = (g * jax.nn.sigmoid(g) * yn).astype(y_ref.dtype)

    @pl.when(c == pl.num_programs(2) - 1)
    def _():
        sout_ref[0, 0] = s_new


def _retention(qkvg, s0, cos, sin, batch, seqlen, pos0, chunk):
    C = chunk
    nc = seqlen // C
    assert seqlen % C == 0 and pos0 % C == 0
    pblk = pos0 // C
    pos = jnp.arange(C, dtype=F32)
    diff = pos[:, None] - pos[None, :]
    causal = diff >= 0
    log_g = jnp.log1p(-jnp.exp2(-5.0 - jnp.arange(RET_HEADS, dtype=F32)))
    dmask = jnp.where(causal, jnp.exp(log_g[:, None, None] * jnp.where(causal, diff, 0.0)), 0.0)
    q_dec = jnp.exp(log_g[:, None] * (pos + 1.0)[None, :])
    k_dec = jnp.exp(log_g[:, None] * (C - 1.0 - pos)[None, :])
    s_dec = jnp.exp(log_g * C)
    qd = jnp.broadcast_to(q_dec[:, :, None], (RET_HEADS, C, RET_DK))
    kd = jnp.broadcast_to(k_dec[:, :, None], (RET_HEADS, C, RET_DK))
    sd = jnp.broadcast_to(s_dec[:, None, None], (RET_HEADS, 1, RET_DV))

    nq = RET_QK // RET_DK
    nv = 2 * RET_QK // RET_DV
    ng = nv + RET_VW // RET_DV
    row = lambda b, h, c: b * nc + c
    y, s_out = pl.pallas_call(
        _retention_kernel,
        grid=(batch, RET_HEADS, nc),
        in_specs=[
            pl.BlockSpec((C, RET_DK), lambda b, h, c: (row(b, h, c), h)),
            pl.BlockSpec((C, RET_DK), lambda b, h, c: (row(b, h, c), nq + h)),
            pl.BlockSpec((C, RET_DV), lambda b, h, c: (row(b, h, c), nv + h)),
            pl.BlockSpec((C, RET_DV), lambda b, h, c: (row(b, h, c), ng + h)),
            pl.BlockSpec((C, RET_DK // 2), lambda b, h, c: (pblk + c, 0)),
            pl.BlockSpec((C, RET_DK // 2), lambda b, h, c: (pblk + c, 0)),
            pl.BlockSpec((1, C, C), lambda b, h, c: (h, 0, 0)),
            pl.BlockSpec((1, C, RET_DK), lambda b, h, c: (h, 0, 0)),
            pl.BlockSpec((1, C, RET_DK), lambda b, h, c: (h, 0, 0)),
            pl.BlockSpec((1, 1, RET_DV), lambda b, h, c: (h, 0, 0)),
            pl.BlockSpec((1, 1, RET_DK, RET_DV), lambda b, h, c: (b, h, 0, 0)),
        ],
        out_specs=[
            pl.BlockSpec((C, RET_DV), lambda b, h, c: (row(b, h, c), h)),
            pl.BlockSpec((1, 1, RET_DK, RET_DV), lambda b, h, c: (b, h, 0, 0)),
        ],
        out_shape=[jax.ShapeDtypeStruct((batch * seqlen, RET_VW), BF16),
                   jax.ShapeDtypeStruct((batch, RET_HEADS, RET_DK, RET_DV), F32)],
        scratch_shapes=[pltpu.VMEM((RET_DK, RET_DV), F32)],
        compiler_params=_cparams("parallel", "parallel", "arbitrary"),
        name="retention",
    )(qkvg, qkvg, qkvg, qkvg, cos, sin, dmask, qd, kd, sd, s0)
    return y, s_out


def _log_sigmoid(z):
    return jnp.minimum(z, 0.0) - jnp.log(1.0 + jnp.exp(-jnp.abs(z)))


def _split3(x):
    hi = x.astype(BF16)
    r1 = x - hi.astype(F32)
    mid = r1.astype(BF16)
    lo = (r1 - mid.astype(F32)).astype(BF16)
    return hi, mid, lo


def _stickbreak_kernel(q_ref, k_ref, v_ref, o_ref, *, pos0, qb, kb):
    qi = pl.program_id(2)
    npair = q_ref.shape[1] // LANES
    q_first = pos0 + qi * qb
    q_pos = q_first + lax.broadcasted_iota(jnp.int32, (qb, 1), 0)
    first_head = lax.broadcasted_iota(jnp.int32, (1, LANES), 1) < SB_HD
    tri = (lax.broadcasted_iota(jnp.int32, (kb, kb), 0)
           > lax.broadcasted_iota(jnp.int32, (kb, kb), 1)).astype(BF16)
    qs = []
    for p in range(npair):
        qp = q_ref[:, p * LANES:(p + 1) * LANES] * (SB_HD ** -0.5)
        qs += [jnp.where(first_head, qp, 0.0).astype(BF16), jnp.where(first_head, 0.0, qp).astype(BF16)]
    start_blk = (q_first + qb - 2) // kb

    def cond(st):
        blk, carries, _ = st
        worst = functools.reduce(jnp.maximum, carries)
        return jnp.logical_and(blk >= 0, jnp.max(worst) > -LOG_F32_UNDERFLOW)

    def body(st):
        blk, carries, accs = st
        off = pl.multiple_of(blk * kb, kb)
        k_pos = off + lax.broadcasted_iota(jnp.int32, (1, kb), 1)
        mask = k_pos < q_pos
        heads = range(2 * npair)
        ks = [k_ref[pl.ds(off, kb), p * LANES:(p + 1) * LANES].astype(BF16) for p in range(npair)]
        vs = [v_ref[pl.ds(off, kb), p * LANES:(p + 1) * LANES].astype(BF16) for p in range(npair)]
        z = [lax.dot_general(qs[h], ks[h // 2], (((1,), (1,)), ((), ())),
                             preferred_element_type=F32) for h in heads]
        ls = [_log_sigmoid(z[h]) for h in heads]
        lr = [jnp.where(mask, ls[h] - z[h], 0.0) for h in heads]
        suffix = []
        for h in heads:
            hi, mid, lo = _split3(lr[h])
            suffix.append(jnp.dot(hi, tri, preferred_element_type=F32)
                          + jnp.dot(mid, tri, preferred_element_type=F32)
                          + jnp.dot(lo, tri, preferred_element_type=F32))
        w = [jnp.where(mask, jnp.exp(ls[h] + (carries[h] + suffix[h])), 0.0).astype(BF16)
             for h in heads]
        pv = [jnp.dot(w[h], vs[h // 2], preferred_element_type=F32) for h in heads]
        new_carries = tuple(carries[h] + jnp.sum(lr[h], axis=-1, keepdims=True) for h in heads)
        new_accs = tuple(accs[p] + jnp.where(first_head, pv[2 * p], pv[2 * p + 1])
                         for p in range(npair))
        return blk - 1, new_carries, new_accs

    init = (start_blk, tuple(jnp.zeros((qb, 1), F32) for _ in range(2 * npair)),
            tuple(jnp.zeros((qb, LANES), F32) for _ in range(npair)))
    _, _, accs = lax.while_loop(cond, body, init)
    for p in range(npair):
        o_ref[:, p * LANES:(p + 1) * LANES] = accs[p]


def _stickbreak(q_arr, q_col, k_arr, k_col, v_arr, v_col, batch, lq, lk, pos0, qb, kb, width):
    assert lq % qb == 0 and lk % kb == 0 and pos0 + lq <= lk
    assert width % LANES == 0 and D_MODEL % width == 0
    assert q_col % width == 0 and k_col % width == 0 and v_col % width == 0
    nq = lq // qb
    return pl.pallas_call(
        functools.partial(_stickbreak_kernel, pos0=pos0, qb=qb, kb=kb),
        grid=(batch, D_MODEL // width, nq),
        in_specs=[pl.BlockSpec((qb, width), lambda b, g, i: (b * nq + i, q_col // width + g)),
                  pl.BlockSpec((lk, width), lambda b, g, i: (b, k_col // width + g)),
                  pl.BlockSpec((lk, width), lambda b, g, i: (b, v_col // width + g))],
        out_specs=pl.BlockSpec((qb, width), lambda b, g, i: (b * nq + i, g)),
        out_shape=jax.ShapeDtypeStruct((batch * lq, D_MODEL), F32),
        compiler_params=_cparams("parallel", "parallel", "arbitrary"),
        name="stickbreak_attention",
    )(q_arr, k_arr, v_arr)


def _top_values(s, n):
    m = jnp.max(s, axis=0)
    vals = [m]
    for _ in range(n - 1):
        m = jnp.max(jnp.where(s < m, s, NEG), axis=0)
        vals.append(m)
    return vals


def _peer_route_kernel(xT_ref, wqT_ref, ka_kh_ref, kb_kh_ref, kb_hk_ref, ea_ref, ta_ref, sb_ref, eb_ref,
                       sa_scr, sbk_scr):
    nq = PEER_HEADS * PEER_QDIM // 2
    tt = xT_ref.shape[1]
    nsel = PEER_TOPK + 1
    q = jnp.dot(wqT_ref[...], xT_ref[...], preferred_element_type=F32).astype(BF16)
    qa = q[:nq]
    qb = q[nq:]
    sa_scr[...] = jnp.dot(ka_kh_ref[...], qa, preferred_element_type=F32)
    sb_ref[...] = jnp.dot(kb_hk_ref[...], qb, preferred_element_type=F32)
    sbk_scr[...] = jnp.dot(kb_kh_ref[...], qb, preferred_element_type=F32)
    for g in range(tt // LANES):
        lanes = slice(g * LANES, (g + 1) * LANES)
        sa3 = sa_scr[:, lanes].reshape(N_KEYS, PEER_HEADS, LANES)
        va = _top_values(sa3, nsel)
        vb = _top_values(sbk_scr[:, lanes].reshape(N_KEYS, PEER_HEADS, LANES), nsel)
        cand = jnp.stack([va[i] + vb[j] for i in range(nsel) for j in range(nsel)
                          if (i + 1) * (j + 1) <= nsel], axis=0)
        tops = _top_values(cand, nsel)
        tau = 0.5 * (tops[PEER_TOPK - 1] + tops[PEER_TOPK])
        zsum = jnp.sum(jnp.where(cand > tau, jnp.exp(cand - (va[0] + vb[0])), 0.0), axis=0)
        ea_ref[:, :, lanes] = jnp.exp(sa3 - va[0][None])
        ta_ref[:, :, lanes] = tau[None] - sa3
        zinv = 1.0 / zsum
        for h in range(PEER_HEADS):
            rows = slice(h * N_KEYS, (h + 1) * N_KEYS)
            eb_ref[rows, lanes] = jnp.exp(sb_ref[rows, lanes] - vb[0][h:h + 1]) * zinv[h:h + 1]


def _peer_route(xT, wqT, ka_kh, kb_kh, kb_hk, tt):
    D, T = xT.shape
    assert T % tt == 0
    rows = PEER_HEADS * N_KEYS
    nq = PEER_HEADS * PEER_QDIM // 2
    return pl.pallas_call(
        _peer_route_kernel,
        grid=(T // tt,),
        in_specs=[pl.BlockSpec((D, tt), lambda i: (0, i)),
                  pl.BlockSpec((2 * nq, D), lambda i: (0, 0)),
                  pl.BlockSpec((rows, nq), lambda i: (0, 0)),
                  pl.BlockSpec((rows, nq), lambda i: (0, 0)),
                  pl.BlockSpec((rows, nq), lambda i: (0, 0))],
        out_specs=[pl.BlockSpec((N_KEYS, PEER_HEADS, tt), lambda i: (0, 0, i))] * 2
                  + [pl.BlockSpec((rows, tt), lambda i: (0, i))] * 2,
        out_shape=[jax.ShapeDtypeStruct((N_KEYS, PEER_HEADS, T), F32)] * 2
                  + [jax.ShapeDtypeStruct((rows, T), F32)] * 2,
        scratch_shapes=[pltpu.VMEM((rows, tt), F32), pltpu.VMEM((rows, tt), F32)],
        compiler_params=_cparams("parallel"),
        name="peer_route",
    )(xT, wqT, ka_kh, kb_kh, kb_hk)


def _peer_weights(w_q, keys_a, keys_b, u, v):
    half = PEER_QDIM // 2
    wqT = w_q.reshape(D_MODEL, PEER_HEADS, 2, half).transpose(2, 1, 3, 0)
    wqT = wqT.reshape(PEER_HEADS * PEER_QDIM, D_MODEL)
    eye = jnp.eye(PEER_HEADS, dtype=keys_a.dtype)
    n = PEER_HEADS * N_KEYS
    ka_kh = jnp.einsum('hkd,hg->khgd', keys_a, eye).reshape(n, PEER_HEADS * half)
    kb_kh = jnp.einsum('hkd,hg->khgd', keys_b, eye).reshape(n, PEER_HEADS * half)
    kb_hk = jnp.einsum('hkd,hg->hkgd', keys_b, eye).reshape(n, PEER_HEADS * half)
    vT = v.reshape(N_EXPERTS // MXU_TILE, MXU_TILE, D_MODEL).transpose(0, 2, 1)
    return tuple(t.astype(BF16) for t in (wqT, ka_kh, kb_kh, kb_hk, u, vT))


def _gelu(x):
    return 0.5 * x * (1.0 + lax.erf(x * (2.0 ** -0.5)))


def _peer_dense_kernel(xT_ref, u_ref, vT_ref, ea_ref, ta_ref, sb_ref, eb_ref, o_ref, acc_ref, w_scr):
    j = pl.program_id(1)
    nu = u_ref.shape[0] // MXU_TILE
    tt = xT_ref.shape[1]
    na = MXU_TILE // N_KEYS

    @pl.when(j == 0)
    def _():
        acc_ref[...] = jnp.zeros_like(acc_ref)
        w_scr[2] = jnp.zeros(w_scr.shape[1:], BF16)

    for tb in range(tt // MXU_TILE):
        tok = slice(tb * MXU_TILE, (tb + 1) * MXU_TILE)

        def unit(n, carry):
            slot = n % 2
            u_rows = pl.ds(pl.multiple_of(n * MXU_TILE, MXU_TILE), MXU_TILE)
            hv = jnp.dot(u_ref[u_rows, :], xT_ref[:, tok], preferred_element_type=F32)
            acc_ref[:, tok] += jnp.dot(vT_ref[jnp.maximum(n - 1, 0)],
                                       w_scr[jnp.where(n == 0, 2, 1 - slot)],
                                       preferred_element_type=F32)
            for g in range(MXU_TILE // LANES):
                lanes = slice(tb * MXU_TILE + g * LANES, tb * MXU_TILE + (g + 1) * LANES)
                cols = slice(g * LANES, (g + 1) * LANES)
                for bs in range(N_KEYS // GATE_ROWS):
                    gate = [jnp.zeros((GATE_ROWS, LANES), F32) for _ in range(na)]
                    for h in range(PEER_HEADS):
                        b_rows = slice(h * N_KEYS + bs * GATE_ROWS, h * N_KEYS + (bs + 1) * GATE_ROWS)
                        sb = sb_ref[b_rows, lanes]
                        eb = eb_ref[b_rows, lanes]
                        for al in range(na):
                            a = (j * nu + n) * na + al
                            gate[al] = gate[al] + jnp.where(sb >= ta_ref[a, h:h + 1, lanes],
                                                            eb * ea_ref[a, h:h + 1, lanes], 0.0)
                    for al in range(na):
                        rows = slice(al * N_KEYS + bs * GATE_ROWS, al * N_KEYS + (bs + 1) * GATE_ROWS)
                        w_scr[slot, rows, cols] = (gate[al] * _gelu(hv[rows, cols])).astype(BF16)
            return carry

        lax.fori_loop(0, nu, unit, 0)
        acc_ref[:, tok] += jnp.dot(vT_ref[nu - 1], w_scr[(nu - 1) % 2], preferred_element_type=F32)

    @pl.when(j == pl.num_programs(1) - 1)
    def _():
        o_ref[...] = acc_ref[...]


def _peer_dense(xT, u, vT, ea, ta, sb, eb, tt, ec):
    D, T = xT.shape
    E = u.shape[0]
    assert T % tt == 0 and E % ec == 0 and ec % MXU_TILE == 0 and tt % MXU_TILE == 0
    rows = PEER_HEADS * N_KEYS
    nu = ec // MXU_TILE
    return pl.pallas_call(
        _peer_dense_kernel,
        grid=(T // tt, E // ec),
        in_specs=[pl.BlockSpec((D, tt), lambda i, j: (0, i)),
                  pl.BlockSpec((ec, D), lambda i, j: (j, 0)),
                  pl.BlockSpec((nu, D, MXU_TILE), lambda i, j: (j, 0, 0))]
                 + [pl.BlockSpec((N_KEYS, PEER_HEADS, tt), lambda i, j: (0, 0, i))] * 2
                 + [pl.BlockSpec((rows, tt), lambda i, j: (0, i))] * 2,
        out_specs=pl.BlockSpec((D, tt), lambda i, j: (0, i)),
        out_shape=jax.ShapeDtypeStruct((D, T), F32),
        scratch_shapes=[pltpu.VMEM((D, tt), F32), pltpu.VMEM((3, MXU_TILE, MXU_TILE), BF16)],
        compiler_params=_cparams("parallel", "arbitrary"),
        name="peer_dense",
    )(xT, u, vT, ea, ta, sb, eb)


def _peer(x, weights, tt=512, ec=2048):
    wqT, ka_kh, kb_kh, kb_hk, u, vT = weights
    tt = min(tt, x.shape[0])
    xT = x.T.astype(BF16)
    ea, ta, sb, eb = _peer_route(xT, wqT, ka_kh, kb_kh, kb_hk, tt)
    return _peer_dense(xT, u, vT, ea, ta, sb, eb, tt, ec).T


def _rope_tables(length):
    half = RET_DK // 2
    inv = ROPE_BASE ** (-jnp.arange(half, dtype=F32) / half)
    ang = jnp.arange(length, dtype=F32)[:, None] * inv[None, :]
    return jnp.cos(ang), jnp.sin(ang)


def _heads_first(t, batch, seqlen):
    return t.reshape(batch, seqlen, SB_HEADS, SB_HD).transpose(0, 2, 1, 3)


def kernel(x_prompt, x_sample, state_ret, cache_k, cache_v, ret_w_in, ret_w_o, sb_w_qkv, sb_w_o,
           peer_w_q, peer_keys_a, peer_keys_b, peer_u, peer_v, ln_g, ln_b):
    B, L, D = x_prompt.shape
    Bs, Ls, _ = x_sample.shape
    xp = x_prompt.reshape(B * L, D)
    xs = x_sample.reshape(Bs * Ls, D)
    cos, sin = _rope_tables(max(L, PAST_LEN + Ls))

    w_in = ret_w_in[0].astype(BF16)
    w_o = ret_w_o[0].astype(BF16)
    s0_p = jnp.zeros((B, RET_HEADS, RET_DK, RET_DV), F32)
    yp, ret_p = _retention(_mm(xp, w_in, F32), s0_p, cos, sin, B, L, 0, 256)
    ys, ret_s = _retention(_mm(xs, w_in, F32), state_ret[0], cos, sin, Bs, Ls, PAST_LEN, Ls)
    xp = _mm_res_ln(yp, w_o, xp, ln_g[0, 0], ln_b[0, 0])
    xs = _mm_res_ln(ys, w_o, xs, ln_g[0, 0], ln_b[0, 0])

    def peer_layer(i, xp, xs):
        weights = _peer_weights(peer_w_q[i], peer_keys_a[i], peer_keys_b[i], peer_u[i], peer_v[i])
        xp = _res_ln(xp, _peer(xp, weights), ln_g[i, 1], ln_b[i, 1])
        xs = _res_ln(xs, _peer(xs, weights), ln_g[i, 1], ln_b[i, 1])
        return xp, xs

    xp, xs = peer_layer(0, xp, xs)

    w_qkv = sb_w_qkv[0].astype(BF16)
    w_o = sb_w_o[0].astype(BF16)
    qkv_p = _mm(xp, w_qkv, F32)
    qkv_s = _mm(xs, w_qkv, F32)
    kp, vp = [_heads_first(qkv_p[:, c * D:(c + 1) * D], B, L) for c in (1, 2)]
    ks, vs = [_heads_first(qkv_s[:, c * D:(c + 1) * D], Bs, Ls) for c in (1, 2)]
    op = _stickbreak(qkv_p, 0, qkv_p, D, qkv_p, 2 * D, B, L, L, 0, 256, 128, 256)
    lk = PAST_LEN + Ls

    def with_past(cache, new_rows):
        past = cache.transpose(0, 2, 1, 3).reshape(Bs, PAST_LEN, D)
        return jnp.concatenate([past, new_rows.reshape(Bs, Ls, D)], axis=1).reshape(Bs * lk, D)

    k_all = with_past(cache_k[0], qkv_s[:, D:2 * D])
    v_all = with_past(cache_v[0], qkv_s[:, 2 * D:])
    os_ = _stickbreak(qkv_s, 0, k_all, 0, v_all, 0, Bs, Ls, lk, PAST_LEN, Ls, Ls, 256)
    xp = _mm_res_ln(op, w_o, xp, ln_g[1, 0], ln_b[1, 0])
    xs = _mm_res_ln(os_, w_o, xs, ln_g[1, 0], ln_b[1, 0])

    xp, xs = peer_layer(1, xp, xs)

    return (xp.reshape(B, L, D), xs.reshape(Bs, Ls, D), ret_p[None], ret_s[None],
            kp[None], vp[None], ks[None], vs[None])
```

```python
import functools

import jax
import jax.numpy as jnp
from jax import lax
from jax.experimental import pallas as pl
from jax.experimental.pallas import tpu as pltpu

D_MODEL = 1024
DEPTH = 2
PAST_LEN = 4096
RET_HEADS = 4
RET_DK = D_MODEL // RET_HEADS
RET_DV = 2 * RET_DK
RET_QK = RET_HEADS * RET_DK
RET_VW = RET_HEADS * RET_DV
ROPE_BASE = 10000.0
SB_HEADS = 16
SB_HD = D_MODEL // SB_HEADS
PEER_HEADS = 8
PEER_QDIM = 256
N_KEYS = 128
N_EXPERTS = N_KEYS * N_KEYS
PEER_TOPK = 16
LN_EPS = 1e-5
GN_EPS = 1e-6
ALPHA = (2 * DEPTH) ** 0.25

LANES = 128
MXU_TILE = 256
PEER_UNIT = 2 * MXU_TILE
GATE_ROWS = 32
VMEM_LIMIT = 48 * 1024 * 1024
NEG = -1e30
LOG_F32_UNDERFLOW = 104.0

BF16 = jnp.bfloat16
F32 = jnp.float32


def _cparams(*sem):
    return pltpu.CompilerParams(dimension_semantics=sem, vmem_limit_bytes=VMEM_LIMIT)


def _mm_kernel(x_ref, w_ref, o_ref):
    o_ref[...] = jnp.dot(x_ref[...].astype(BF16), w_ref[...],
                         preferred_element_type=F32).astype(o_ref.dtype)


def _mm(x, w, out_dtype, tm=512, tn=1024):
    M, K = x.shape
    N = w.shape[1]
    tm = min(tm, M)
    tn = min(tn, N)
    assert M % tm == 0 and N % tn == 0
    return pl.pallas_call(
        _mm_kernel,
        grid=(M // tm, N // tn),
        in_specs=[pl.BlockSpec((tm, K), lambda i, j: (i, 0)),
                  pl.BlockSpec((K, tn), lambda i, j: (0, j))],
        out_specs=pl.BlockSpec((tm, tn), lambda i, j: (i, j)),
        out_shape=jax.ShapeDtypeStruct((M, N), out_dtype),
        compiler_params=_cparams("parallel", "arbitrary"),
        name="proj_matmul",
    )(x, w)


def _layer_norm_rows(r, g, b):
    mu = jnp.mean(r, axis=-1, keepdims=True)
    d = r - mu
    var = jnp.mean(d * d, axis=-1, keepdims=True)
    return d * lax.rsqrt(var + LN_EPS) * g + b


def _mm_res_ln_kernel(a_ref, w_ref, x_ref, g_ref, b_ref, o_ref, oT_ref):
    m = jnp.dot(a_ref[...].astype(BF16), w_ref[...], preferred_element_type=F32)
    y = _layer_norm_rows(ALPHA * x_ref[...] + m, g_ref[...], b_ref[...])
    o_ref[...] = y
    oT_ref[...] = y.T.astype(BF16)


def _mm_res_ln(a, w, x, g, b, tm=512):
    M, K = a.shape
    D = w.shape[1]
    tm = min(tm, M)
    assert M % tm == 0
    return pl.pallas_call(
        _mm_res_ln_kernel,
        grid=(M // tm,),
        in_specs=[pl.BlockSpec((tm, K), lambda i: (i, 0)),
                  pl.BlockSpec((K, D), lambda i: (0, 0)),
                  pl.BlockSpec((tm, D), lambda i: (i, 0)),
                  pl.BlockSpec((1, D), lambda i: (0, 0)),
                  pl.BlockSpec((1, D), lambda i: (0, 0))],
        out_specs=[pl.BlockSpec((tm, D), lambda i: (i, 0)),
                   pl.BlockSpec((D, tm), lambda i: (0, i))],
        out_shape=[jax.ShapeDtypeStruct((M, D), F32), jax.ShapeDtypeStruct((D, M), BF16)],
        compiler_params=_cparams("parallel"),
        name="out_proj_residual_ln",
    )(a, w, x, g.reshape(1, D), b.reshape(1, D))


def _retention_kernel(q_ref, k_ref, v_ref, g_ref, cos_ref, sin_ref, dm_ref, qd_ref, kd_ref,
                      sd_ref, s0_ref, y_ref, sout_ref, s_scr):
    c = pl.program_id(2)

    @pl.when(c == 0)
    def _():
        s_scr[...] = s0_ref[0, 0]

    cos = cos_ref[...]
    sin = sin_ref[...]
    half = RET_DK // 2

    def rot(x):
        x1 = x[:, :half]
        x2 = x[:, half:]
        return jnp.concatenate([x1 * cos - x2 * sin, x1 * sin + x2 * cos], axis=-1)

    qr = rot(q_ref[...])
    kr = rot(k_ref[...]) * (RET_DK ** -0.5)
    v = v_ref[...].astype(BF16)
    S = s_scr[...]

    scores = lax.dot_general(qr.astype(BF16), kr.astype(BF16), (((1,), (1,)), ((), ())),
                             preferred_element_type=F32)
    inner = jnp.dot((scores * dm_ref[0]).astype(BF16), v, preferred_element_type=F32)
    cross = jnp.dot((qr * qd_ref[0]).astype(BF16), S.astype(BF16), preferred_element_type=F32)
    o = inner + cross

    kdec = (kr * kd_ref[0]).astype(BF16)
    s_new = S * sd_ref[0] + lax.dot_general(kdec, v, (((0,), (0,)), ((), ())),
                                            preferred_element_type=F32)
    s_scr[...] = s_new

    mu = jnp.mean(o, axis=-1, keepdims=True)
    d = o - mu
    var = jnp.mean(d * d, axis=-1, keepdims=True)
    yn = d * lax.rsqrt(var + GN_EPS)
    g = g_ref[...]
    y_ref[...] = (g * jax.nn.sigmoid(g) * yn).astype(y_ref.dtype)

    @pl.when(c == pl.num_programs(2) - 1)
    def _():
        sout_ref[0, 0] = s_new


def _retention(qkvg, s0, cos, sin, batch, seqlen, pos0, chunk):
    C = chunk
    nc = seqlen // C
    assert seqlen % C == 0 and pos0 % C == 0
    pblk = pos0 // C
    pos = jnp.arange(C, dtype=F32)
    diff = pos[:, None] - pos[None, :]
    causal = diff >= 0
    log_g = jnp.log1p(-jnp.exp2(-5.0 - jnp.arange(RET_HEADS, dtype=F32)))
    dmask = jnp.where(causal, jnp.exp(log_g[:, None, None] * jnp.where(causal, diff, 0.0)), 0.0)
    q_dec = jnp.exp(log_g[:, None] * (pos + 1.0)[None, :])
    k_dec = jnp.exp(log_g[:, None] * (C - 1.0 - pos)[None, :])
    s_dec = jnp.exp(log_g * C)
    qd = jnp.broadcast_to(q_dec[:, :, None], (RET_HEADS, C, RET_DK))
    kd = jnp.broadcast_to(k_dec[:, :, None], (RET_HEADS, C, RET_DK))
    sd = jnp.broadcast_to(s_dec[:, None, None], (RET_HEADS, 1, RET_DV))

    nq = RET_QK // RET_DK
    nv = 2 * RET_QK // RET_DV
    ng = nv + RET_VW // RET_DV
    row = lambda b, h, c: b * nc + c
    y, s_out = pl.pallas_call(
        _retention_kernel,
        grid=(batch, RET_HEADS, nc),
        in_specs=[
            pl.BlockSpec((C, RET_DK), lambda b, h, c: (row(b, h, c), h)),
            pl.BlockSpec((C, RET_DK), lambda b, h, c: (row(b, h, c), nq + h)),
            pl.BlockSpec((C, RET_DV), lambda b, h, c: (row(b, h, c), nv + h)),
            pl.BlockSpec((C, RET_DV), lambda b, h, c: (row(b, h, c), ng + h)),
            pl.BlockSpec((C, RET_DK // 2), lambda b, h, c: (pblk + c, 0)),
            pl.BlockSpec((C, RET_DK // 2), lambda b, h, c: (pblk + c, 0)),
            pl.BlockSpec((1, C, C), lambda b, h, c: (h, 0, 0)),
            pl.BlockSpec((1, C, RET_DK), lambda b, h, c: (h, 0, 0)),
            pl.BlockSpec((1, C, RET_DK), lambda b, h, c: (h, 0, 0)),
            pl.BlockSpec((1, 1, RET_DV), lambda b, h, c: (h, 0, 0)),
            pl.BlockSpec((1, 1, RET_DK, RET_DV), lambda b, h, c: (b, h, 0, 0)),
        ],
        out_specs=[
            pl.BlockSpec((C, RET_DV), lambda b, h, c: (row(b, h, c), h)),
            pl.BlockSpec((1, 1, RET_DK, RET_DV), lambda b, h, c: (b, h, 0, 0)),
        ],
        out_shape=[jax.ShapeDtypeStruct((batch * seqlen, RET_VW), BF16),
                   jax.ShapeDtypeStruct((batch, RET_HEADS, RET_DK, RET_DV), F32)],
        scratch_shapes=[pltpu.VMEM((RET_DK, RET_DV), F32)],
        compiler_params=_cparams("parallel", "parallel", "arbitrary"),
        name="retention",
    )(qkvg, qkvg, qkvg, qkvg, cos, sin, dmask, qd, kd, sd, s0)
    return y, s_out


def _log_sigmoid(z):
    return jnp.minimum(z, 0.0) - jnp.log(1.0 + jnp.exp(-jnp.abs(z)))


def _split3(x):
    hi = x.astype(BF16)
    r1 = x - hi.astype(F32)
    mid = r1.astype(BF16)
    lo = (r1 - mid.astype(F32)).astype(BF16)
    return hi, mid, lo


def _stickbreak_kernel(q_ref, k_ref, v_ref, o_ref, *, pos0, qb, kb):
    qi = pl.program_id(2)
    npair = q_ref.shape[1] // LANES
    q_first = pos0 + qi * qb
    q_pos = q_first + lax.broadcasted_iota(jnp.int32, (qb, 1), 0)
    first_head = lax.broadcasted_iota(jnp.int32, (1, LANES), 1) < SB_HD
    tri = (lax.broadcasted_iota(jnp.int32, (kb, kb), 0)
           > lax.broadcasted_iota(jnp.int32, (kb, kb), 1)).astype(BF16)
    qs = []
    for p in range(npair):
        qp = q_ref[:, p * LANES:(p + 1) * LANES] * (SB_HD ** -0.5)
        qs += [jnp.where(first_head, qp, 0.0).astype(BF16), jnp.where(first_head, 0.0, qp).astype(BF16)]
    start_blk = (q_first + qb - 2) // kb

    def cond(st):
        blk, carries, _ = st
        worst = functools.reduce(jnp.maximum, carries)
        return jnp.logical_and(blk >= 0, jnp.max(worst) > -LOG_F32_UNDERFLOW)

    def body(st):
        blk, carries, accs = st
        off = pl.multiple_of(blk * kb, kb)
        k_pos = off + lax.broadcasted_iota(jnp.int32, (1, kb), 1)
        mask = k_pos < q_pos
        heads = range(2 * npair)
        ks = [k_ref[pl.ds(off, kb), p * LANES:(p + 1) * LANES].astype(BF16) for p in range(npair)]
        vs = [v_ref[pl.ds(off, kb), p * LANES:(p + 1) * LANES].astype(BF16) for p in range(npair)]
        z = [lax.dot_general(qs[h], ks[h // 2], (((1,), (1,)), ((), ())),
                             preferred_element_type=F32) for h in heads]
        ls = [_log_sigmoid(z[h]) for h in heads]
        lr = [jnp.where(mask, ls[h] - z[h], 0.0) for h in heads]
        suffix = []
        for h in heads:
            hi, mid, lo = _split3(lr[h])
            suffix.append(jnp.dot(hi, tri, preferred_element_type=F32)
                          + jnp.dot(mid, tri, preferred_element_type=F32)
                          + jnp.dot(lo, tri, preferred_element_type=F32))
        w = [jnp.where(mask, jnp.exp(ls[h] + (carries[h] + suffix[h])), 0.0).astype(BF16)
             for h in heads]
        pv = [jnp.dot(w[h], vs[h // 2], preferred_element_type=F32) for h in heads]
        new_carries = tuple(carries[h] + jnp.sum(lr[h], axis=-1, keepdims=True) for h in heads)
        new_accs = tuple(accs[p] + jnp.where(first_head, pv[2 * p], pv[2 * p + 1])
                         for p in range(npair))
        return blk - 1, new_carries, new_accs

    init = (start_blk, tuple(jnp.zeros((qb, 1), F32) for _ in range(2 * npair)),
            tuple(jnp.zeros((qb, LANES), F32) for _ in range(npair)))
    _, _, accs = lax.while_loop(cond, body, init)
    for p in range(npair):
        o_ref[:, p * LANES:(p + 1) * LANES] = accs[p]


def _stickbreak(q_arr, q_col, k_arr, k_col, v_arr, v_col, batch, lq, lk, pos0, qb, kb, width):
    assert lq % qb == 0 and lk % kb == 0 and pos0 + lq <= lk
    assert width % LANES == 0 and D_MODEL % width == 0
    assert q_col % width == 0 and k_col % width == 0 and v_col % width == 0
    nq = lq // qb
    return pl.pallas_call(
        functools.partial(_stickbreak_kernel, pos0=pos0, qb=qb, kb=kb),
        grid=(batch, D_MODEL // width, nq),
        in_specs=[pl.BlockSpec((qb, width), lambda b, g, i: (b * nq + i, q_col // width + g)),
                  pl.BlockSpec((lk, width), lambda b, g, i: (b, k_col // width + g)),
                  pl.BlockSpec((lk, width), lambda b, g, i: (b, v_col // width + g))],
        out_specs=pl.BlockSpec((qb, width), lambda b, g, i: (b * nq + i, g)),
        out_shape=jax.ShapeDtypeStruct((batch * lq, D_MODEL), F32),
        compiler_params=_cparams("parallel", "parallel", "arbitrary"),
        name="stickbreak_attention",
    )(q_arr, k_arr, v_arr)


def _top_values(s, n):
    m = jnp.max(s, axis=0)
    vals = [m]
    for _ in range(n - 1):
        m = jnp.max(jnp.where(s < m, s, NEG), axis=0)
        vals.append(m)
    return vals


def _peer_route_kernel(xT_ref, wqT_ref, ka_kh_ref, kb_kh_ref, kb_hk_ref, ea_ref, ta_ref, sb_ref, eb_ref,
                       sa_scr, sbk_scr):
    nq = PEER_HEADS * PEER_QDIM // 2
    tt = xT_ref.shape[1]
    nsel = PEER_TOPK + 1
    q = jnp.dot(wqT_ref[...], xT_ref[...], preferred_element_type=F32).astype(BF16)
    qa = q[:nq]
    qb = q[nq:]
    sa_scr[...] = jnp.dot(ka_kh_ref[...], qa, preferred_element_type=F32)
    sb_ref[...] = jnp.dot(kb_hk_ref[...], qb, preferred_element_type=F32)
    sbk_scr[...] = jnp.dot(kb_kh_ref[...], qb, preferred_element_type=F32)
    for g in range(tt // LANES):
        lanes = slice(g * LANES, (g + 1) * LANES)
        sa3 = sa_scr[:, lanes].reshape(N_KEYS, PEER_HEADS, LANES)
        va = _top_values(sa3, nsel)
        vb = _top_values(sbk_scr[:, lanes].reshape(N_KEYS, PEER_HEADS, LANES), nsel)
        cand = jnp.stack([va[i] + vb[j] for i in range(nsel) for j in range(nsel)
                          if (i + 1) * (j + 1) <= nsel], axis=0)
        tops = _top_values(cand, nsel)
        tau = 0.5 * (tops[PEER_TOPK - 1] + tops[PEER_TOPK])
        zsum = jnp.sum(jnp.where(cand > tau, jnp.exp(cand - (va[0] + vb[0])), 0.0), axis=0)
        ea_ref[:, :, lanes] = jnp.exp(sa3 - va[0][None])
        ta_ref[:, :, lanes] = tau[None] - sa3
        zinv = 0.5 / zsum
        for h in range(PEER_HEADS):
            rows = slice(h * N_KEYS, (h + 1) * N_KEYS)
            eb_ref[rows, lanes] = jnp.exp(sb_ref[rows, lanes] - vb[0][h:h + 1]) * zinv[h:h + 1]


def _peer_route(xT, wqT, ka_kh, kb_kh, kb_hk, tt):
    D, T = xT.shape
    assert T % tt == 0
    rows = PEER_HEADS * N_KEYS
    nq = PEER_HEADS * PEER_QDIM // 2
    return pl.pallas_call(
        _peer_route_kernel,
        grid=(T // tt,),
        in_specs=[pl.BlockSpec((D, tt), lambda i: (0, i)),
                  pl.BlockSpec((2 * nq, D), lambda i: (0, 0)),
                  pl.BlockSpec((rows, nq), lambda i: (0, 0)),
                  pl.BlockSpec((rows, nq), lambda i: (0, 0)),
                  pl.BlockSpec((rows, nq), lambda i: (0, 0))],
        out_specs=[pl.BlockSpec((N_KEYS, PEER_HEADS, tt), lambda i: (0, 0, i))] * 2
                  + [pl.BlockSpec((rows, tt), lambda i: (0, i))] * 2,
        out_shape=[jax.ShapeDtypeStruct((N_KEYS, PEER_HEADS, T), F32)] * 2
                  + [jax.ShapeDtypeStruct((rows, T), F32)] * 2,
        scratch_shapes=[pltpu.VMEM((rows, tt), F32), pltpu.VMEM((rows, tt), F32)],
        compiler_params=_cparams("parallel"),
        name="peer_route",
    )(xT, wqT, ka_kh, kb_kh, kb_hk)


def _peer_weights(w_q, keys_a, keys_b, u, v):
    half = PEER_QDIM // 2
    wqT = w_q.reshape(D_MODEL, PEER_HEADS, 2, half).transpose(2, 1, 3, 0)
    wqT = wqT.reshape(PEER_HEADS * PEER_QDIM, D_MODEL)
    eye = jnp.eye(PEER_HEADS, dtype=keys_a.dtype)
    n = PEER_HEADS * N_KEYS
    ka_kh = jnp.einsum('hkd,hg->khgd', keys_a, eye).reshape(n, PEER_HEADS * half)
    kb_kh = jnp.einsum('hkd,hg->khgd', keys_b, eye).reshape(n, PEER_HEADS * half)
    kb_hk = jnp.einsum('hkd,hg->hkgd', keys_b, eye).reshape(n, PEER_HEADS * half)
    vT = v.reshape(N_EXPERTS // PEER_UNIT, PEER_UNIT, D_MODEL).transpose(0, 2, 1)
    return tuple(t.astype(BF16) for t in (wqT, ka_kh, kb_kh, kb_hk, u, vT))


def _gelu_x2(x):
    return x * (1.0 + lax.erf(x * (2.0 ** -0.5)))


def _peer_dense_kernel(xT_ref, u_ref, vT_ref, ea_ref, ta_ref, sb_ref, eb_ref, x_ref, g_ref, b_ref,
                       o_ref, acc_ref, w_scr):
    j = pl.program_id(1)
    nu = u_ref.shape[0] // PEER_UNIT
    tt = xT_ref.shape[1]
    na = PEER_UNIT // N_KEYS

    @pl.when(j == 0)
    def _():
        acc_ref[...] = jnp.zeros_like(acc_ref)
        w_scr[2] = jnp.zeros(w_scr.shape[1:], BF16)

    for tb in range(tt // MXU_TILE):
        tok = slice(tb * MXU_TILE, (tb + 1) * MXU_TILE)

        def unit(n, carry):
            slot = n % 2
            u_rows = pl.ds(pl.multiple_of(n * PEER_UNIT, PEER_UNIT), PEER_UNIT)
            hv = jnp.dot(u_ref[u_rows, :], xT_ref[:, tok], preferred_element_type=F32)
            acc_ref[:, tok] += jnp.dot(vT_ref[jnp.maximum(n - 1, 0)],
                                       w_scr[jnp.where(n == 0, 2, 1 - slot)],
                                       preferred_element_type=F32)
            for g in range(MXU_TILE // LANES):
                lanes = slice(tb * MXU_TILE + g * LANES, tb * MXU_TILE + (g + 1) * LANES)
                cols = slice(g * LANES, (g + 1) * LANES)
                for bs in range(N_KEYS // GATE_ROWS):
                    gate = [jnp.zeros((GATE_ROWS, LANES), F32) for _ in range(na)]
                    for h in range(PEER_HEADS):
                        b_rows = slice(h * N_KEYS + bs * GATE_ROWS, h * N_KEYS + (bs + 1) * GATE_ROWS)
                        sb = sb_ref[b_rows, lanes]
                        eb = eb_ref[b_rows, lanes]
                        for al in range(na):
                            a = (j * nu + n) * na + al
                            gate[al] = gate[al] + jnp.where(sb >= ta_ref[a, h:h + 1, lanes],
                                                            eb * ea_ref[a, h:h + 1, lanes], 0.0)
                    for al in range(na):
                        rows = slice(al * N_KEYS + bs * GATE_ROWS, al * N_KEYS + (bs + 1) * GATE_ROWS)
                        w_scr[slot, rows, cols] = (gate[al] * _gelu_x2(hv[rows, cols])).astype(BF16)
            return carry

        lax.fori_loop(0, nu, unit, 0)
        acc_ref[:, tok] += jnp.dot(vT_ref[nu - 1], w_scr[(nu - 1) % 2], preferred_element_type=F32)

    @pl.when(j == pl.num_programs(1) - 1)
    def _():
        o_ref[...] = _layer_norm_rows(ALPHA * x_ref[...] + acc_ref[...].T, g_ref[...], b_ref[...])


def _peer_dense(x, xT, u, vT, ea, ta, sb, eb, g, b, tt, ec):
    D, T = xT.shape
    E = u.shape[0]
    assert T % tt == 0 and E % ec == 0 and ec % PEER_UNIT == 0 and tt % MXU_TILE == 0
    rows = PEER_HEADS * N_KEYS
    nu = ec // PEER_UNIT
    return pl.pallas_call(
        _peer_dense_kernel,
        grid=(T // tt, E // ec),
        in_specs=[pl.BlockSpec((D, tt), lambda i, j: (0, i)),
                  pl.BlockSpec((ec, D), lambda i, j: (j, 0)),
                  pl.BlockSpec((nu, D, PEER_UNIT), lambda i, j: (j, 0, 0))]
                 + [pl.BlockSpec((N_KEYS, PEER_HEADS, tt), lambda i, j: (0, 0, i))] * 2
                 + [pl.BlockSpec((rows, tt), lambda i, j: (0, i))] * 2
                 + [pl.BlockSpec((tt, D), lambda i, j: (i, 0)),
                    pl.BlockSpec((1, D), lambda i, j: (0, 0)),
                    pl.BlockSpec((1, D), lambda i, j: (0, 0))],
        out_specs=pl.BlockSpec((tt, D), lambda i, j: (i, 0)),
        out_shape=jax.ShapeDtypeStruct((T, D), F32),
        scratch_shapes=[pltpu.VMEM((D, tt), F32), pltpu.VMEM((3, PEER_UNIT, MXU_TILE), BF16)],
        compiler_params=_cparams("parallel", "arbitrary"),
        name="peer_dense",
    )(xT, u, vT, ea, ta, sb, eb, x, g.reshape(1, D), b.reshape(1, D))


def _peer_ln(x, xT, weights, g, b, tt=512, ec=2048):
    wqT, ka_kh, kb_kh, kb_hk, u, vT = weights
    tt = min(tt, x.shape[0])
    ea, ta, sb, eb = _peer_route(xT, wqT, ka_kh, kb_kh, kb_hk, tt)
    return _peer_dense(x, xT, u, vT, ea, ta, sb, eb, g, b, tt, ec)


def _rope_tables(length):
    half = RET_DK // 2
    inv = ROPE_BASE ** (-jnp.arange(half, dtype=F32) / half)
    ang = jnp.arange(length, dtype=F32)[:, None] * inv[None, :]
    return jnp.cos(ang), jnp.sin(ang)


def _heads_first(t, batch, seqlen):
    return t.reshape(batch, seqlen, SB_HEADS, SB_HD).transpose(0, 2, 1, 3)


def kernel(x_prompt, x_sample, state_ret, cache_k, cache_v, ret_w_in, ret_w_o, sb_w_qkv, sb_w_o,
           peer_w_q, peer_keys_a, peer_keys_b, peer_u, peer_v, ln_g, ln_b):
    B, L, D = x_prompt.shape
    Bs, Ls, _ = x_sample.shape
    xp = x_prompt.reshape(B * L, D)
    xs = x_sample.reshape(Bs * Ls, D)
    cos, sin = _rope_tables(max(L, PAST_LEN + Ls))

    w_in = ret_w_in[0].astype(BF16)
    w_o = ret_w_o[0].astype(BF16)
    s0_p = jnp.zeros((B, RET_HEADS, RET_DK, RET_DV), F32)
    yp, ret_p = _retention(_mm(xp, w_in, F32), s0_p, cos, sin, B, L, 0, 256)
    ys, ret_s = _retention(_mm(xs, w_in, F32), state_ret[0], cos, sin, Bs, Ls, PAST_LEN, Ls)
    xp, xpT = _mm_res_ln(yp, w_o, xp, ln_g[0, 0], ln_b[0, 0])
    xs, xsT = _mm_res_ln(ys, w_o, xs, ln_g[0, 0], ln_b[0, 0])

    def peer_layer(i, xp, xpT, xs, xsT):
        weights = _peer_weights(peer_w_q[i], peer_keys_a[i], peer_keys_b[i], peer_u[i], peer_v[i])
        xp = _peer_ln(xp, xpT, weights, ln_g[i, 1], ln_b[i, 1])
        xs = _peer_ln(xs, xsT, weights, ln_g[i, 1], ln_b[i, 1])
        return xp, xs

    xp, xs = peer_layer(0, xp, xpT, xs, xsT)

    w_qkv = sb_w_qkv[0].astype(BF16)
    w_o = sb_w_o[0].astype(BF16)
    qkv_p = _mm(xp, w_qkv, F32)
    qkv_s = _mm(xs, w_qkv, F32)
    kp, vp = [_heads_first(qkv_p[:, c * D:(c + 1) * D], B, L) for c in (1, 2)]
    ks, vs = [_heads_first(qkv_s[:, c * D:(c + 1) * D], Bs, Ls) for c in (1, 2)]
    op = _stickbreak(qkv_p, 0, qkv_p, D, qkv_p, 2 * D, B, L, L, 0, 256, 128, 256)
    lk = PAST_LEN + Ls

    def with_past(cache, new_rows):
        past = cache.transpose(0, 2, 1, 3).reshape(Bs, PAST_LEN, D)
        return jnp.concatenate([past, new_rows.reshape(Bs, Ls, D)], axis=1).reshape(Bs * lk, D)

    k_all = with_past(cache_k[0], qkv_s[:, D:2 * D])
    v_all = with_past(cache_v[0], qkv_s[:, 2 * D:])
    os_ = _stickbreak(qkv_s, 0, k_all, 0, v_all, 0, Bs, Ls, lk, PAST_LEN, Ls, Ls, 256)
    xp, xpT = _mm_res_ln(op, w_o, xp, ln_g[1, 0], ln_b[1, 0])
    xs, xsT = _mm_res_ln(os_, w_o, xs, ln_g[1, 0], ln_b[1, 0])

    xp, xs = peer_layer(1, xp, xpT, xs, xsT)

    return (xp.reshape(B, L, D), xs.reshape(Bs, Ls, D), ret_p[None], ret_s[None],
            kp[None], vp[None], ks[None], vs[None])
```

```python
import functools

import jax
import jax.numpy as jnp
from jax import lax
from jax.experimental import pallas as pl
from jax.experimental.pallas import tpu as pltpu

D_MODEL = 1024
DEPTH = 2
PAST_LEN = 4096
RET_HEADS = 4
RET_DK = D_MODEL // RET_HEADS
RET_DV = 2 * RET_DK
RET_QK = RET_HEADS * RET_DK
RET_VW = RET_HEADS * RET_DV
ROPE_BASE = 10000.0
SB_HEADS = 16
SB_HD = D_MODEL // SB_HEADS
PEER_HEADS = 8
PEER_QDIM = 256
N_KEYS = 128
N_EXPERTS = N_KEYS * N_KEYS
PEER_TOPK = 16
LN_EPS = 1e-5
GN_EPS = 1e-6
ALPHA = (2 * DEPTH) ** 0.25

LANES = 128
MXU_TILE = 256
PEER_UNIT = 2 * MXU_TILE
GATE_ROWS = 32
VMEM_LIMIT = 48 * 1024 * 1024
NEG = -1e30
LOG_F32_UNDERFLOW = 104.0

BF16 = jnp.bfloat16
F32 = jnp.float32


def _cparams(*sem):
    return pltpu.CompilerParams(dimension_semantics=sem, vmem_limit_bytes=VMEM_LIMIT)


def _mm_kernel(x_ref, w_ref, o_ref):
    o_ref[...] = jnp.dot(x_ref[...].astype(BF16), w_ref[...],
                         preferred_element_type=F32).astype(o_ref.dtype)


def _mm(x, w, out_dtype, tm=512, tn=1024):
    M, K = x.shape
    N = w.shape[1]
    tm = min(tm, M)
    tn = min(tn, N)
    assert M % tm == 0 and N % tn == 0
    return pl.pallas_call(
        _mm_kernel,
        grid=(M // tm, N // tn),
        in_specs=[pl.BlockSpec((tm, K), lambda i, j: (i, 0)),
                  pl.BlockSpec((K, tn), lambda i, j: (0, j))],
        out_specs=pl.BlockSpec((tm, tn), lambda i, j: (i, j)),
        out_shape=jax.ShapeDtypeStruct((M, N), out_dtype),
        compiler_params=_cparams("parallel", "arbitrary"),
        name="proj_matmul",
    )(x, w)


def _layer_norm_rows(r, g, b):
    mu = jnp.mean(r, axis=-1, keepdims=True)
    d = r - mu
    var = jnp.mean(d * d, axis=-1, keepdims=True)
    return d * lax.rsqrt(var + LN_EPS) * g + b


def _mm_res_ln_kernel(a_ref, w_ref, x_ref, g_ref, b_ref, o_ref, oT_ref):
    m = jnp.dot(a_ref[...].astype(BF16), w_ref[...], preferred_element_type=F32)
    y = _layer_norm_rows(ALPHA * x_ref[...] + m, g_ref[...], b_ref[...])
    o_ref[...] = y
    oT_ref[...] = y.T.astype(BF16)


def _mm_res_ln(a, w, x, g, b, tm=512):
    M, K = a.shape
    D = w.shape[1]
    tm = min(tm, M)
    assert M % tm == 0
    return pl.pallas_call(
        _mm_res_ln_kernel,
        grid=(M // tm,),
        in_specs=[pl.BlockSpec((tm, K), lambda i: (i, 0)),
                  pl.BlockSpec((K, D), lambda i: (0, 0)),
                  pl.BlockSpec((tm, D), lambda i: (i, 0)),
                  pl.BlockSpec((1, D), lambda i: (0, 0)),
                  pl.BlockSpec((1, D), lambda i: (0, 0))],
        out_specs=[pl.BlockSpec((tm, D), lambda i: (i, 0)),
                   pl.BlockSpec((D, tm), lambda i: (0, i))],
        out_shape=[jax.ShapeDtypeStruct((M, D), F32), jax.ShapeDtypeStruct((D, M), BF16)],
        compiler_params=_cparams("parallel"),
        name="out_proj_residual_ln",
    )(a, w, x, g.reshape(1, D), b.reshape(1, D))


def _retention_kernel(q_ref, k_ref, v_ref, g_ref, cos_ref, sin_ref, dm_ref, qd_ref, kd_ref,
                      sd_ref, s0_ref, y_ref, sout_ref, s_scr):
    c = pl.program_id(2)

    @pl.when(c == 0)
    def _():
        s_scr[...] = s0_ref[0, 0]

    cos = cos_ref[...]
    sin = sin_ref[...]
    half = RET_DK // 2

    def rot(x):
        x1 = x[:, :half]
        x2 = x[:, half:]
        return jnp.concatenate([x1 * cos - x2 * sin, x1 * sin + x2 * cos], axis=-1)

    qr = rot(q_ref[...])
    kr = rot(k_ref[...]) * (RET_DK ** -0.5)
    v = v_ref[...].astype(BF16)
    S = s_scr[...]

    scores = lax.dot_general(qr.astype(BF16), kr.astype(BF16), (((1,), (1,)), ((), ())),
                             preferred_element_type=F32)
    inner = jnp.dot((scores * dm_ref[0]).astype(BF16), v, preferred_element_type=F32)
    cross = jnp.dot((qr * qd_ref[0]).astype(BF16), S.astype(BF16), preferred_element_type=F32)
    o = inner + cross

    kdec = (kr * kd_ref[0]).astype(BF16)
    s_new = S * sd_ref[0] + lax.dot_general(kdec, v, (((0,), (0,)), ((), ())),
                                            preferred_element_type=F32)
    s_scr[...] = s_new

    mu = jnp.mean(o, axis=-1, keepdims=True)
    d = o - mu
    var = jnp.mean(d * d, axis=-1, keepdims=True)
    yn = d * lax.rsqrt(var + GN_EPS)
    g = g_ref[...]
    y_ref[...] = (g * jax.nn.sigmoid(g) * yn).astype(y_ref.dtype)

    @pl.when(c == pl.num_programs(2) - 1)
    def _():
        sout_ref[0, 0] = s_new


def _retention(qkvg, s0, cos, sin, batch, seqlen, pos0, chunk):
    C = chunk
    nc = seqlen // C
    assert seqlen % C == 0 and pos0 % C == 0
    pblk = pos0 // C
    pos = jnp.arange(C, dtype=F32)
    diff = pos[:, None] - pos[None, :]
    causal = diff >= 0
    log_g = jnp.log1p(-jnp.exp2(-5.0 - jnp.arange(RET_HEADS, dtype=F32)))
    dmask = jnp.where(causal, jnp.exp(log_g[:, None, None] * jnp.where(causal, diff, 0.0)), 0.0)
    q_dec = jnp.exp(log_g[:, None] * (pos + 1.0)[None, :])
    k_dec = jnp.exp(log_g[:, None] * (C - 1.0 - pos)[None, :])
    s_dec = jnp.exp(log_g * C)
    qd = jnp.broadcast_to(q_dec[:, :, None], (RET_HEADS, C, RET_DK))
    kd = jnp.broadcast_to(k_dec[:, :, None], (RET_HEADS, C, RET_DK))
    sd = jnp.broadcast_to(s_dec[:, None, None], (RET_HEADS, 1, RET_DV))

    nq = RET_QK // RET_DK
    nv = 2 * RET_QK // RET_DV
    ng = nv + RET_VW // RET_DV
    row = lambda b, h, c: b * nc + c
    y, s_out = pl.pallas_call(
        _retention_kernel,
        grid=(batch, RET_HEADS, nc),
        in_specs=[
            pl.BlockSpec((C, RET_DK), lambda b, h, c: (row(b, h, c), h)),
            pl.BlockSpec((C, RET_DK), lambda b, h, c: (row(b, h, c), nq + h)),
            pl.BlockSpec((C, RET_DV), lambda b, h, c: (row(b, h, c), nv + h)),
            pl.BlockSpec((C, RET_DV), lambda b, h, c: (row(b, h, c), ng + h)),
            pl.BlockSpec((C, RET_DK // 2), lambda b, h, c: (pblk + c, 0)),
            pl.BlockSpec((C, RET_DK // 2), lambda b, h, c: (pblk + c, 0)),
            pl.BlockSpec((1, C, C), lambda b, h, c: (h, 0, 0)),
            pl.BlockSpec((1, C, RET_DK), lambda b, h, c: (h, 0, 0)),
            pl.BlockSpec((1, C, RET_DK), lambda b, h, c: (h, 0, 0)),
            pl.BlockSpec((1, 1, RET_DV), lambda b, h, c: (h, 0, 0)),
            pl.BlockSpec((1, 1, RET_DK, RET_DV), lambda b, h, c: (b, h, 0, 0)),
        ],
        out_specs=[
            pl.BlockSpec((C, RET_DV), lambda b, h, c: (row(b, h, c), h)),
            pl.BlockSpec((1, 1, RET_DK, RET_DV), lambda b, h, c: (b, h, 0, 0)),
        ],
        out_shape=[jax.ShapeDtypeStruct((batch * seqlen, RET_VW), BF16),
                   jax.ShapeDtypeStruct((batch, RET_HEADS, RET_DK, RET_DV), F32)],
        scratch_shapes=[pltpu.VMEM((RET_DK, RET_DV), F32)],
        compiler_params=_cparams("parallel", "parallel", "arbitrary"),
        name="retention",
    )(qkvg, qkvg, qkvg, qkvg, cos, sin, dmask, qd, kd, sd, s0)
    return y, s_out


def _log_sigmoid(z):
    return jnp.minimum(z, 0.0) - jnp.log(1.0 + jnp.exp(-jnp.abs(z)))


def _split3(x):
    hi = x.astype(BF16)
    r1 = x - hi.astype(F32)
    mid = r1.astype(BF16)
    lo = (r1 - mid.astype(F32)).astype(BF16)
    return hi, mid, lo


def _stickbreak_kernel(q_ref, k_ref, v_ref, o_ref, *, pos0, qb, kb):
    qi = pl.program_id(2)
    npair = q_ref.shape[1] // LANES
    q_first = pos0 + qi * qb
    q_pos = q_first + lax.broadcasted_iota(jnp.int32, (qb, 1), 0)
    first_head = lax.broadcasted_iota(jnp.int32, (1, LANES), 1) < SB_HD
    tri = (lax.broadcasted_iota(jnp.int32, (kb, kb), 0)
           > lax.broadcasted_iota(jnp.int32, (kb, kb), 1)).astype(BF16)
    qs = []
    for p in range(npair):
        qp = q_ref[:, p * LANES:(p + 1) * LANES] * (SB_HD ** -0.5)
        qs += [jnp.where(first_head, qp, 0.0).astype(BF16), jnp.where(first_head, 0.0, qp).astype(BF16)]
    start_blk = (q_first + qb - 2) // kb

    def cond(st):
        blk, carries, _ = st
        worst = functools.reduce(jnp.maximum, carries)
        return jnp.logical_and(blk >= 0, jnp.max(worst) > -LOG_F32_UNDERFLOW)

    def body(st):
        blk, carries, accs = st
        off = pl.multiple_of(blk * kb, kb)
        k_pos = off + lax.broadcasted_iota(jnp.int32, (1, kb), 1)
        mask = k_pos < q_pos
        heads = range(2 * npair)
        ks = [k_ref[pl.ds(off, kb), p * LANES:(p + 1) * LANES].astype(BF16) for p in range(npair)]
        vs = [v_ref[pl.ds(off, kb), p * LANES:(p + 1) * LANES].astype(BF16) for p in range(npair)]
        z = [lax.dot_general(qs[h], ks[h // 2], (((1,), (1,)), ((), ())),
                             preferred_element_type=F32) for h in heads]
        ls = [_log_sigmoid(z[h]) for h in heads]
        lr = [jnp.where(mask, ls[h] - z[h], 0.0) for h in heads]
        suffix = []
        for h in heads:
            hi, mid, lo = _split3(lr[h])
            suffix.append(jnp.dot(hi, tri, preferred_element_type=F32)
                          + jnp.dot(mid, tri, preferred_element_type=F32)
                          + jnp.dot(lo, tri, preferred_element_type=F32))
        w = [jnp.where(mask, jnp.exp(ls[h] + (carries[h] + suffix[h])), 0.0).astype(BF16)
             for h in heads]
        pv = [jnp.dot(w[h], vs[h // 2], preferred_element_type=F32) for h in heads]
        new_carries = tuple(carries[h] + jnp.sum(lr[h], axis=-1, keepdims=True) for h in heads)
        new_accs = tuple(accs[p] + jnp.where(first_head, pv[2 * p], pv[2 * p + 1])
                         for p in range(npair))
        return blk - 1, new_carries, new_accs

    init = (start_blk, tuple(jnp.zeros((qb, 1), F32) for _ in range(2 * npair)),
            tuple(jnp.zeros((qb, LANES), F32) for _ in range(npair)))
    _, _, accs = lax.while_loop(cond, body, init)
    for p in range(npair):
        o_ref[:, p * LANES:(p + 1) * LANES] = accs[p]


def _stickbreak(q_arr, q_col, k_arr, k_col, v_arr, v_col, batch, lq, lk, pos0, qb, kb, width):
    assert lq % qb == 0 and lk % kb == 0 and pos0 + lq <= lk
    assert width % LANES == 0 and D_MODEL % width == 0
    assert q_col % width == 0 and k_col % width == 0 and v_col % width == 0
    nq = lq // qb
    return pl.pallas_call(
        functools.partial(_stickbreak_kernel, pos0=pos0, qb=qb, kb=kb),
        grid=(batch, D_MODEL // width, nq),
        in_specs=[pl.BlockSpec((qb, width), lambda b, g, i: (b * nq + i, q_col // width + g)),
                  pl.BlockSpec((lk, width), lambda b, g, i: (b, k_col // width + g)),
                  pl.BlockSpec((lk, width), lambda b, g, i: (b, v_col // width + g))],
        out_specs=pl.BlockSpec((qb, width), lambda b, g, i: (b * nq + i, g)),
        out_shape=jax.ShapeDtypeStruct((batch * lq, D_MODEL), F32),
        compiler_params=_cparams("parallel", "parallel", "arbitrary"),
        name="stickbreak_attention",
    )(q_arr, k_arr, v_arr)


def _top_values(s, n):
    m = jnp.max(s, axis=0)
    vals = [m]
    for _ in range(n - 1):
        m = jnp.max(jnp.where(s < m, s, NEG), axis=0)
        vals.append(m)
    return vals


def _peer_route_kernel(xT_ref, wqT_ref, ka_kh_ref, kb_kh_ref, kb_hk_ref, ea_ref, ta_ref, sb_ref, eb_ref,
                       sa_scr, sbk_scr):
    nq = PEER_HEADS * PEER_QDIM // 2
    tt = xT_ref.shape[1]
    nsel = PEER_TOPK + 1
    q = jnp.dot(wqT_ref[...], xT_ref[...], preferred_element_type=F32).astype(BF16)
    qa = q[:nq]
    qb = q[nq:]
    sa_scr[...] = jnp.dot(ka_kh_ref[...], qa, preferred_element_type=F32)
    sb_ref[...] = jnp.dot(kb_hk_ref[...], qb, preferred_element_type=F32)
    sbk_scr[...] = jnp.dot(kb_kh_ref[...], qb, preferred_element_type=F32)
    for g in range(tt // LANES):
        lanes = slice(g * LANES, (g + 1) * LANES)
        sa3 = sa_scr[:, lanes].reshape(N_KEYS, PEER_HEADS, LANES)
        va = _top_values(sa3, nsel)
        vb = _top_values(sbk_scr[:, lanes].reshape(N_KEYS, PEER_HEADS, LANES), nsel)
        cand = jnp.stack([va[i] + vb[j] for i in range(nsel) for j in range(nsel)
                          if (i + 1) * (j + 1) <= nsel], axis=0)
        tops = _top_values(cand, nsel)
        tau = 0.5 * (tops[PEER_TOPK - 1] + tops[PEER_TOPK])
        zsum = jnp.sum(jnp.where(cand > tau, jnp.exp(cand - (va[0] + vb[0])), 0.0), axis=0)
        ea_ref[:, :, lanes] = jnp.exp(sa3 - va[0][None])
        ta_ref[:, :, lanes] = tau[None] - sa3
        zinv = 0.5 / zsum
        for h in range(PEER_HEADS):
            rows = slice(h * N_KEYS, (h + 1) * N_KEYS)
            eb_ref[rows, lanes] = jnp.exp(sb_ref[rows, lanes] - vb[0][h:h + 1]) * zinv[h:h + 1]


def _peer_route(xT, wqT, ka_kh, kb_kh, kb_hk, tt):
    D, T = xT.shape
    assert T % tt == 0
    rows = PEER_HEADS * N_KEYS
    nq = PEER_HEADS * PEER_QDIM // 2
    return pl.pallas_call(
        _peer_route_kernel,
        grid=(T // tt,),
        in_specs=[pl.BlockSpec((D, tt), lambda i: (0, i)),
                  pl.BlockSpec((2 * nq, D), lambda i: (0, 0)),
                  pl.BlockSpec((rows, nq), lambda i: (0, 0)),
                  pl.BlockSpec((rows, nq), lambda i: (0, 0)),
                  pl.BlockSpec((rows, nq), lambda i: (0, 0))],
        out_specs=[pl.BlockSpec((N_KEYS, PEER_HEADS, tt), lambda i: (0, 0, i))] * 2
                  + [pl.BlockSpec((rows, tt), lambda i: (0, i))] * 2,
        out_shape=[jax.ShapeDtypeStruct((N_KEYS, PEER_HEADS, T), F32)] * 2
                  + [jax.ShapeDtypeStruct((rows, T), F32)] * 2,
        scratch_shapes=[pltpu.VMEM((rows, tt), F32), pltpu.VMEM((rows, tt), F32)],
        compiler_params=_cparams("parallel"),
        name="peer_route",
    )(xT, wqT, ka_kh, kb_kh, kb_hk)


def _peer_weights(w_q, keys_a, keys_b, u, v):
    half = PEER_QDIM // 2
    wqT = w_q.reshape(D_MODEL, PEER_HEADS, 2, half).transpose(2, 1, 3, 0)
    wqT = wqT.reshape(PEER_HEADS * PEER_QDIM, D_MODEL)
    eye = jnp.eye(PEER_HEADS, dtype=keys_a.dtype)
    n = PEER_HEADS * N_KEYS
    ka_kh = jnp.einsum('hkd,hg->khgd', keys_a, eye).reshape(n, PEER_HEADS * half)
    kb_kh = jnp.einsum('hkd,hg->khgd', keys_b, eye).reshape(n, PEER_HEADS * half)
    kb_hk = jnp.einsum('hkd,hg->hkgd', keys_b, eye).reshape(n, PEER_HEADS * half)
    vT = v.reshape(N_EXPERTS // PEER_UNIT, PEER_UNIT, D_MODEL).transpose(0, 2, 1)
    return tuple(t.astype(BF16) for t in (wqT, ka_kh, kb_kh, kb_hk, u, vT))


def _gelu_x2(x):
    return x * (1.0 + lax.erf(x * (2.0 ** -0.5)))


def _peer_dense_kernel(xT_ref, u_ref, vT_ref, ea_ref, ta_ref, sb_ref, eb_ref, x_ref, g_ref, b_ref,
                       o_ref, acc_ref, w_scr):
    j = pl.program_id(1)
    nu = u_ref.shape[0] // PEER_UNIT
    tt = xT_ref.shape[1]
    na = PEER_UNIT // N_KEYS

    @pl.when(j == 0)
    def _():
        acc_ref[...] = jnp.zeros_like(acc_ref)
        w_scr[2] = jnp.zeros(w_scr.shape[1:], BF16)

    for tb in range(tt // MXU_TILE):
        tok = slice(tb * MXU_TILE, (tb + 1) * MXU_TILE)

        def unit(n, carry):
            slot = n % 2
            u_rows = pl.ds(pl.multiple_of(n * PEER_UNIT, PEER_UNIT), PEER_UNIT)
            hv = jnp.dot(u_ref[u_rows, :], xT_ref[:, tok], preferred_element_type=F32)
            acc_ref[:, tok] += jnp.dot(vT_ref[jnp.maximum(n - 1, 0)],
                                       w_scr[jnp.where(n == 0, 2, 1 - slot)],
                                       preferred_element_type=F32)
            for g in range(MXU_TILE // LANES):
                lanes = slice(tb * MXU_TILE + g * LANES, tb * MXU_TILE + (g + 1) * LANES)
                cols = slice(g * LANES, (g + 1) * LANES)
                for bs in range(N_KEYS // GATE_ROWS):
                    gate = [jnp.zeros((GATE_ROWS, LANES), F32) for _ in range(na)]
                    for h in range(PEER_HEADS):
                        b_rows = slice(h * N_KEYS + bs * GATE_ROWS, h * N_KEYS + (bs + 1) * GATE_ROWS)
                        sb = sb_ref[b_rows, lanes]
                        eb = eb_ref[b_rows, lanes]
                        for al in range(na):
                            a = (j * nu + n) * na + al
                            gate[al] = gate[al] + jnp.where(sb >= ta_ref[a, h:h + 1, lanes],
                                                            eb * ea_ref[a, h:h + 1, lanes], 0.0)
                    for al in range(na):
                        rows = slice(al * N_KEYS + bs * GATE_ROWS, al * N_KEYS + (bs + 1) * GATE_ROWS)
                        w_scr[slot, rows, cols] = (gate[al] * _gelu_x2(hv[rows, cols])).astype(BF16)
            return carry

        lax.fori_loop(0, nu, unit, 0)
        acc_ref[:, tok] += jnp.dot(vT_ref[nu - 1], w_scr[(nu - 1) % 2], preferred_element_type=F32)

    @pl.when(j == pl.num_programs(1) - 1)
    def _():
        o_ref[...] = _layer_norm_rows(ALPHA * x_ref[...] + acc_ref[...].T, g_ref[...], b_ref[...])


def _peer_dense(x, xT, u, vT, ea, ta, sb, eb, g, b, tt, ec):
    D, T = xT.shape
    E = u.shape[0]
    assert T % tt == 0 and E % ec == 0 and ec % PEER_UNIT == 0 and tt % MXU_TILE == 0
    rows = PEER_HEADS * N_KEYS
    nu = ec // PEER_UNIT
    return pl.pallas_call(
        _peer_dense_kernel,
        grid=(T // tt, E // ec),
        in_specs=[pl.BlockSpec((D, tt), lambda i, j: (0, i)),
                  pl.BlockSpec((ec, D), lambda i, j: (j, 0)),
                  pl.BlockSpec((nu, D, PEER_UNIT), lambda i, j: (j, 0, 0))]
                 + [pl.BlockSpec((N_KEYS, PEER_HEADS, tt), lambda i, j: (0, 0, i))] * 2
                 + [pl.BlockSpec((rows, tt), lambda i, j: (0, i))] * 2
                 + [pl.BlockSpec((tt, D), lambda i, j: (i, 0)),
                    pl.BlockSpec((1, D), lambda i, j: (0, 0)),
                    pl.BlockSpec((1, D), lambda i, j: (0, 0))],
        out_specs=pl.BlockSpec((tt, D), lambda i, j: (i, 0)),
        out_shape=jax.ShapeDtypeStruct((T, D), F32),
        scratch_shapes=[pltpu.VMEM((D, tt), F32), pltpu.VMEM((3, PEER_UNIT, MXU_TILE), BF16)],
        compiler_params=_cparams("parallel", "arbitrary"),
        name="peer_dense",
    )(xT, u, vT, ea, ta, sb, eb, x, g.reshape(1, D), b.reshape(1, D))


def _peer_ln(x, xT, weights, g, b, tt=512, ec=2048):
    wqT, ka_kh, kb_kh, kb_hk, u, vT = weights
    tt = min(tt, x.shape[0])
    ea, ta, sb, eb = _peer_route(xT, wqT, ka_kh, kb_kh, kb_hk, tt)
    return _peer_dense(x, xT, u, vT, ea, ta, sb, eb, g, b, tt, ec)


def _rope_tables(length):
    half = RET_DK // 2
    inv = ROPE_BASE ** (-jnp.arange(half, dtype=F32) / half)
    ang = jnp.arange(length, dtype=F32)[:, None] * inv[None, :]
    return jnp.cos(ang), jnp.sin(ang)


def _heads_first(t, batch, seqlen):
    return t.reshape(batch, seqlen, SB_HEADS, SB_HD).transpose(0, 2, 1, 3)


def kernel(x_prompt, x_sample, state_ret, cache_k, cache_v, ret_w_in, ret_w_o, sb_w_qkv, sb_w_o,
           peer_w_q, peer_keys_a, peer_keys_b, peer_u, peer_v, ln_g, ln_b):
    B, L, D = x_prompt.shape
    Bs, Ls, _ = x_sample.shape
    xp = x_prompt.reshape(B * L, D)
    xs = x_sample.reshape(Bs * Ls, D)
    cos, sin = _rope_tables(max(L, PAST_LEN + Ls))

    w_in = ret_w_in[0].astype(BF16)
    w_o = ret_w_o[0].astype(BF16)
    s0_p = jnp.zeros((B, RET_HEADS, RET_DK, RET_DV), F32)
    yp, ret_p = _retention(_mm(xp, w_in, F32, tm=1024), s0_p, cos, sin, B, L, 0, 256)
    ys, ret_s = _retention(_mm(xs, w_in, F32), state_ret[0], cos, sin, Bs, Ls, PAST_LEN, Ls)
    xp, xpT = _mm_res_ln(yp, w_o, xp, ln_g[0, 0], ln_b[0, 0])
    xs, xsT = _mm_res_ln(ys, w_o, xs, ln_g[0, 0], ln_b[0, 0])

    def peer_layer(i, xp, xpT, xs, xsT):
        weights = _peer_weights(peer_w_q[i], peer_keys_a[i], peer_keys_b[i], peer_u[i], peer_v[i])
        xp = _peer_ln(xp, xpT, weights, ln_g[i, 1], ln_b[i, 1])
        xs = _peer_ln(xs, xsT, weights, ln_g[i, 1], ln_b[i, 1])
        return xp, xs

    xp, xs = peer_layer(0, xp, xpT, xs, xsT)

    w_qkv = sb_w_qkv[0].astype(BF16)
    w_o = sb_w_o[0].astype(BF16)
    qkv_p = _mm(xp, w_qkv, F32, tm=1024)
    qkv_s = _mm(xs, w_qkv, F32)
    kp, vp = [_heads_first(qkv_p[:, c * D:(c + 1) * D], B, L) for c in (1, 2)]
    ks, vs = [_heads_first(qkv_s[:, c * D:(c + 1) * D], Bs, Ls) for c in (1, 2)]
    op = _stickbreak(qkv_p, 0, qkv_p, D, qkv_p, 2 * D, B, L, L, 0, 256, 256, 256)
    lk = PAST_LEN + Ls

    def with_past(cache, new_rows):
        past = cache.transpose(0, 2, 1, 3).reshape(Bs, PAST_LEN, D)
        return jnp.concatenate([past, new_rows.reshape(Bs, Ls, D)], axis=1).reshape(Bs * lk, D)

    k_all = with_past(cache_k[0], qkv_s[:, D:2 * D])
    v_all = with_past(cache_v[0], qkv_s[:, 2 * D:])
    os_ = _stickbreak(qkv_s, 0, k_all, 0, v_all, 0, Bs, Ls, lk, PAST_LEN, Ls, Ls, 256)
    xp, xpT = _mm_res_ln(op, w_o, xp, ln_g[1, 0], ln_b[1, 0])
    xs, xsT = _mm_res_ln(os_, w_o, xs, ln_g[1, 0], ln_b[1, 0])

    xp, xs = peer_layer(1, xp, xpT, xs, xsT)

    return (xp.reshape(B, L, D), xs.reshape(Bs, Ls, D), ret_p[None], ret_s[None],
            kp[None], vp[None], ks[None], vs[None])
```

```python
import functools

import jax
import jax.numpy as jnp
from jax import lax
from jax.experimental import pallas as pl
from jax.experimental.pallas import tpu as pltpu

D_MODEL = 1024
DEPTH = 2
PAST_LEN = 4096
RET_HEADS = 4
RET_DK = D_MODEL // RET_HEADS
RET_DV = 2 * RET_DK
RET_QK = RET_HEADS * RET_DK
RET_VW = RET_HEADS * RET_DV
ROPE_BASE = 10000.0
SB_HEADS = 16
SB_HD = D_MODEL // SB_HEADS
PEER_HEADS = 8
PEER_QDIM = 256
N_KEYS = 128
N_EXPERTS = N_KEYS * N_KEYS
PEER_TOPK = 16
LN_EPS = 1e-5
GN_EPS = 1e-6
ALPHA = (2 * DEPTH) ** 0.25

LANES = 128
MXU_TILE = 256
PEER_UNIT = 2 * MXU_TILE
GATE_ROWS = 32
VMEM_LIMIT = 48 * 1024 * 1024
NEG = -1e30
LOG_F32_UNDERFLOW = 104.0

BF16 = jnp.bfloat16
F32 = jnp.float32


def _cparams(*sem):
    return pltpu.CompilerParams(dimension_semantics=sem, vmem_limit_bytes=VMEM_LIMIT)


def _mm_kernel(x_ref, w_ref, o_ref):
    o_ref[...] = jnp.dot(x_ref[...].astype(BF16), w_ref[...],
                         preferred_element_type=F32).astype(o_ref.dtype)


def _mm(x, w, out_dtype, tm=512, tn=1024):
    M, K = x.shape
    N = w.shape[1]
    tm = min(tm, M)
    tn = min(tn, N)
    assert M % tm == 0 and N % tn == 0
    return pl.pallas_call(
        _mm_kernel,
        grid=(M // tm, N // tn),
        in_specs=[pl.BlockSpec((tm, K), lambda i, j: (i, 0)),
                  pl.BlockSpec((K, tn), lambda i, j: (0, j))],
        out_specs=pl.BlockSpec((tm, tn), lambda i, j: (i, j)),
        out_shape=jax.ShapeDtypeStruct((M, N), out_dtype),
        compiler_params=_cparams("parallel", "arbitrary"),
        name="proj_matmul",
    )(x, w)


def _layer_norm_rows(r, g, b):
    mu = jnp.mean(r, axis=-1, keepdims=True)
    d = r - mu
    var = jnp.mean(d * d, axis=-1, keepdims=True)
    return d * lax.rsqrt(var + LN_EPS) * g + b


def _mm_res_ln_kernel(a_ref, w_ref, x_ref, g_ref, b_ref, o_ref, oT_ref):
    m = jnp.dot(a_ref[...].astype(BF16), w_ref[...], preferred_element_type=F32)
    y = _layer_norm_rows(ALPHA * x_ref[...] + m, g_ref[...], b_ref[...])
    o_ref[...] = y
    oT_ref[...] = y.T.astype(BF16)


def _mm_res_ln(a, w, x, g, b, tm=1024):
    M, K = a.shape
    D = w.shape[1]
    tm = min(tm, M)
    assert M % tm == 0
    return pl.pallas_call(
        _mm_res_ln_kernel,
        grid=(M // tm,),
        in_specs=[pl.BlockSpec((tm, K), lambda i: (i, 0)),
                  pl.BlockSpec((K, D), lambda i: (0, 0)),
                  pl.BlockSpec((tm, D), lambda i: (i, 0)),
                  pl.BlockSpec((1, D), lambda i: (0, 0)),
                  pl.BlockSpec((1, D), lambda i: (0, 0))],
        out_specs=[pl.BlockSpec((tm, D), lambda i: (i, 0)),
                   pl.BlockSpec((D, tm), lambda i: (0, i))],
        out_shape=[jax.ShapeDtypeStruct((M, D), F32), jax.ShapeDtypeStruct((D, M), BF16)],
        compiler_params=_cparams("parallel"),
        name="out_proj_residual_ln",
    )(a, w, x, g.reshape(1, D), b.reshape(1, D))


def _retention_kernel(q_ref, k_ref, v_ref, g_ref, cos_ref, sin_ref, dm_ref, qd_ref, kd_ref,
                      sd_ref, s0_ref, y_ref, sout_ref, s_scr):
    c = pl.program_id(2)

    @pl.when(c == 0)
    def _():
        s_scr[...] = s0_ref[0, 0]

    cos = cos_ref[...]
    sin = sin_ref[...]
    half = RET_DK // 2

    def rot(x):
        x1 = x[:, :half]
        x2 = x[:, half:]
        return jnp.concatenate([x1 * cos - x2 * sin, x1 * sin + x2 * cos], axis=-1)

    qr = rot(q_ref[...])
    kr = rot(k_ref[...]) * (RET_DK ** -0.5)
    v = v_ref[...].astype(BF16)
    S = s_scr[...]

    scores = lax.dot_general(qr.astype(BF16), kr.astype(BF16), (((1,), (1,)), ((), ())),
                             preferred_element_type=F32)
    inner = jnp.dot((scores * dm_ref[0]).astype(BF16), v, preferred_element_type=F32)
    cross = jnp.dot((qr * qd_ref[0]).astype(BF16), S.astype(BF16), preferred_element_type=F32)
    o = inner + cross

    kdec = (kr * kd_ref[0]).astype(BF16)
    s_new = S * sd_ref[0] + lax.dot_general(kdec, v, (((0,), (0,)), ((), ())),
                                            preferred_element_type=F32)
    s_scr[...] = s_new

    mu = jnp.mean(o, axis=-1, keepdims=True)
    d = o - mu
    var = jnp.mean(d * d, axis=-1, keepdims=True)
    yn = d * lax.rsqrt(var + GN_EPS)
    g = g_ref[...]
    y_ref[...] = (g * jax.nn.sigmoid(g) * yn).astype(y_ref.dtype)

    @pl.when(c == pl.num_programs(2) - 1)
    def _():
        sout_ref[0, 0] = s_new


def _retention(qkvg, s0, cos, sin, batch, seqlen, pos0, chunk):
    C = chunk
    nc = seqlen // C
    assert seqlen % C == 0 and pos0 % C == 0
    pblk = pos0 // C
    pos = jnp.arange(C, dtype=F32)
    diff = pos[:, None] - pos[None, :]
    causal = diff >= 0
    log_g = jnp.log1p(-jnp.exp2(-5.0 - jnp.arange(RET_HEADS, dtype=F32)))
    dmask = jnp.where(causal, jnp.exp(log_g[:, None, None] * jnp.where(causal, diff, 0.0)), 0.0)
    q_dec = jnp.exp(log_g[:, None] * (pos + 1.0)[None, :])
    k_dec = jnp.exp(log_g[:, None] * (C - 1.0 - pos)[None, :])
    s_dec = jnp.exp(log_g * C)
    qd = jnp.broadcast_to(q_dec[:, :, None], (RET_HEADS, C, RET_DK))
    kd = jnp.broadcast_to(k_dec[:, :, None], (RET_HEADS, C, RET_DK))
    sd = jnp.broadcast_to(s_dec[:, None, None], (RET_HEADS, 1, RET_DV))

    nq = RET_QK // RET_DK
    nv = 2 * RET_QK // RET_DV
    ng = nv + RET_VW // RET_DV
    row = lambda b, h, c: b * nc + c
    y, s_out = pl.pallas_call(
        _retention_kernel,
        grid=(batch, RET_HEADS, nc),
        in_specs=[
            pl.BlockSpec((C, RET_DK), lambda b, h, c: (row(b, h, c), h)),
            pl.BlockSpec((C, RET_DK), lambda b, h, c: (row(b, h, c), nq + h)),
            pl.BlockSpec((C, RET_DV), lambda b, h, c: (row(b, h, c), nv + h)),
            pl.BlockSpec((C, RET_DV), lambda b, h, c: (row(b, h, c), ng + h)),
            pl.BlockSpec((C, RET_DK // 2), lambda b, h, c: (pblk + c, 0)),
            pl.BlockSpec((C, RET_DK // 2), lambda b, h, c: (pblk + c, 0)),
            pl.BlockSpec((1, C, C), lambda b, h, c: (h, 0, 0)),
            pl.BlockSpec((1, C, RET_DK), lambda b, h, c: (h, 0, 0)),
            pl.BlockSpec((1, C, RET_DK), lambda b, h, c: (h, 0, 0)),
            pl.BlockSpec((1, 1, RET_DV), lambda b, h, c: (h, 0, 0)),
            pl.BlockSpec((1, 1, RET_DK, RET_DV), lambda b, h, c: (b, h, 0, 0)),
        ],
        out_specs=[
            pl.BlockSpec((C, RET_DV), lambda b, h, c: (row(b, h, c), h)),
            pl.BlockSpec((1, 1, RET_DK, RET_DV), lambda b, h, c: (b, h, 0, 0)),
        ],
        out_shape=[jax.ShapeDtypeStruct((batch * seqlen, RET_VW), BF16),
                   jax.ShapeDtypeStruct((batch, RET_HEADS, RET_DK, RET_DV), F32)],
        scratch_shapes=[pltpu.VMEM((RET_DK, RET_DV), F32)],
        compiler_params=_cparams("parallel", "parallel", "arbitrary"),
        name="retention",
    )(qkvg, qkvg, qkvg, qkvg, cos, sin, dmask, qd, kd, sd, s0)
    return y, s_out


def _log_sigmoid(z):
    return jnp.minimum(z, 0.0) - jnp.log(1.0 + jnp.exp(-jnp.abs(z)))


def _split3(x):
    hi = x.astype(BF16)
    r1 = x - hi.astype(F32)
    mid = r1.astype(BF16)
    lo = (r1 - mid.astype(F32)).astype(BF16)
    return hi, mid, lo


def _stickbreak_kernel(q_ref, k_ref, v_ref, o_ref, *, pos0, qb, kb):
    qi = pl.program_id(2)
    npair = q_ref.shape[1] // LANES
    q_first = pos0 + qi * qb
    q_pos = q_first + lax.broadcasted_iota(jnp.int32, (qb, 1), 0)
    first_head = lax.broadcasted_iota(jnp.int32, (1, LANES), 1) < SB_HD
    tri = (lax.broadcasted_iota(jnp.int32, (kb, kb), 0)
           > lax.broadcasted_iota(jnp.int32, (kb, kb), 1)).astype(BF16)
    qs = []
    for p in range(npair):
        qp = q_ref[:, p * LANES:(p + 1) * LANES] * (SB_HD ** -0.5)
        qs += [jnp.where(first_head, qp, 0.0).astype(BF16), jnp.where(first_head, 0.0, qp).astype(BF16)]
    start_blk = (q_first + qb - 2) // kb

    def cond(st):
        blk, carries, _ = st
        worst = functools.reduce(jnp.maximum, carries)
        return jnp.logical_and(blk >= 0, jnp.max(worst) > -LOG_F32_UNDERFLOW)

    def body(st):
        blk, carries, accs = st
        off = pl.multiple_of(blk * kb, kb)
        k_pos = off + lax.broadcasted_iota(jnp.int32, (1, kb), 1)
        mask = k_pos < q_pos
        heads = range(2 * npair)
        ks = [k_ref[pl.ds(off, kb), p * LANES:(p + 1) * LANES].astype(BF16) for p in range(npair)]
        vs = [v_ref[pl.ds(off, kb), p * LANES:(p + 1) * LANES].astype(BF16) for p in range(npair)]
        z = [lax.dot_general(qs[h], ks[h // 2], (((1,), (1,)), ((), ())),
                             preferred_element_type=F32) for h in heads]
        ls = [_log_sigmoid(z[h]) for h in heads]
        lr = [jnp.where(mask, ls[h] - z[h], 0.0) for h in heads]
        suffix = []
        for h in heads:
            hi, mid, lo = _split3(lr[h])
            suffix.append(jnp.dot(hi, tri, preferred_element_type=F32)
                          + jnp.dot(mid, tri, preferred_element_type=F32)
                          + jnp.dot(lo, tri, preferred_element_type=F32))
        w = [jnp.where(mask, jnp.exp(ls[h] + (carries[h] + suffix[h])), 0.0).astype(BF16)
             for h in heads]
        pv = [jnp.dot(w[h], vs[h // 2], preferred_element_type=F32) for h in heads]
        new_carries = tuple(carries[h] + jnp.sum(lr[h], axis=-1, keepdims=True) for h in heads)
        new_accs = tuple(accs[p] + jnp.where(first_head, pv[2 * p], pv[2 * p + 1])
                         for p in range(npair))
        return blk - 1, new_carries, new_accs

    init = (start_blk, tuple(jnp.zeros((qb, 1), F32) for _ in range(2 * npair)),
            tuple(jnp.zeros((qb, LANES), F32) for _ in range(npair)))
    _, _, accs = lax.while_loop(cond, body, init)
    for p in range(npair):
        o_ref[:, p * LANES:(p + 1) * LANES] = accs[p]


def _stickbreak(q_arr, q_col, k_arr, k_col, v_arr, v_col, batch, lq, lk, pos0, qb, kb, width):
    assert lq % qb == 0 and lk % kb == 0 and pos0 + lq <= lk
    assert width % LANES == 0 and D_MODEL % width == 0
    assert q_col % width == 0 and k_col % width == 0 and v_col % width == 0
    nq = lq // qb
    return pl.pallas_call(
        functools.partial(_stickbreak_kernel, pos0=pos0, qb=qb, kb=kb),
        grid=(batch, D_MODEL // width, nq),
        in_specs=[pl.BlockSpec((qb, width), lambda b, g, i: (b * nq + i, q_col // width + g)),
                  pl.BlockSpec((lk, width), lambda b, g, i: (b, k_col // width + g)),
                  pl.BlockSpec((lk, width), lambda b, g, i: (b, v_col // width + g))],
        out_specs=pl.BlockSpec((qb, width), lambda b, g, i: (b * nq + i, g)),
        out_shape=jax.ShapeDtypeStruct((batch * lq, D_MODEL), F32),
        compiler_params=_cparams("parallel", "parallel", "arbitrary"),
        name="stickbreak_attention",
    )(q_arr, k_arr, v_arr)


def _top_values(s, n):
    m = jnp.max(s, axis=0)
    vals = [m]
    for _ in range(n - 1):
        m = jnp.max(jnp.where(s < m, s, NEG), axis=0)
        vals.append(m)
    return vals


def _peer_route_kernel(xT_ref, wqT_ref, ka_kh_ref, kb_kh_ref, kb_hk_ref, ea_ref, ta_ref, sb_ref, eb_ref,
                       sa_scr, sbk_scr):
    nq = PEER_HEADS * PEER_QDIM // 2
    tt = xT_ref.shape[1]
    nsel = PEER_TOPK + 1
    q = jnp.dot(wqT_ref[...], xT_ref[...], preferred_element_type=F32).astype(BF16)
    qa = q[:nq]
    qb = q[nq:]
    sa_scr[...] = jnp.dot(ka_kh_ref[...], qa, preferred_element_type=F32)
    sb_ref[...] = jnp.dot(kb_hk_ref[...], qb, preferred_element_type=F32)
    sbk_scr[...] = jnp.dot(kb_kh_ref[...], qb, preferred_element_type=F32)
    for g in range(tt // LANES):
        lanes = slice(g * LANES, (g + 1) * LANES)
        sa3 = sa_scr[:, lanes].reshape(N_KEYS, PEER_HEADS, LANES)
        va = _top_values(sa3, nsel)
        vb = _top_values(sbk_scr[:, lanes].reshape(N_KEYS, PEER_HEADS, LANES), nsel)
        cand = jnp.stack([va[i] + vb[j] for i in range(nsel) for j in range(nsel)
                          if (i + 1) * (j + 1) <= nsel], axis=0)
        tops = _top_values(cand, nsel)
        tau = 0.5 * (tops[PEER_TOPK - 1] + tops[PEER_TOPK])
        zsum = jnp.sum(jnp.where(cand > tau, jnp.exp(cand - (va[0] + vb[0])), 0.0), axis=0)
        ea_ref[:, :, lanes] = jnp.exp(sa3 - va[0][None])
        ta_ref[:, :, lanes] = tau[None] - sa3
        zinv = 0.5 / zsum
        for h in range(PEER_HEADS):
            rows = slice(h * N_KEYS, (h + 1) * N_KEYS)
            eb_ref[rows, lanes] = jnp.exp(sb_ref[rows, lanes] - vb[0][h:h + 1]) * zinv[h:h + 1]


def _peer_route(xT, wqT, ka_kh, kb_kh, kb_hk, tt):
    D, T = xT.shape
    assert T % tt == 0
    rows = PEER_HEADS * N_KEYS
    nq = PEER_HEADS * PEER_QDIM // 2
    return pl.pallas_call(
        _peer_route_kernel,
        grid=(T // tt,),
        in_specs=[pl.BlockSpec((D, tt), lambda i: (0, i)),
                  pl.BlockSpec((2 * nq, D), lambda i: (0, 0)),
                  pl.BlockSpec((rows, nq), lambda i: (0, 0)),
                  pl.BlockSpec((rows, nq), lambda i: (0, 0)),
                  pl.BlockSpec((rows, nq), lambda i: (0, 0))],
        out_specs=[pl.BlockSpec((N_KEYS, PEER_HEADS, tt), lambda i: (0, 0, i))] * 2
                  + [pl.BlockSpec((rows, tt), lambda i: (0, i))] * 2,
        out_shape=[jax.ShapeDtypeStruct((N_KEYS, PEER_HEADS, T), F32)] * 2
                  + [jax.ShapeDtypeStruct((rows, T), F32)] * 2,
        scratch_shapes=[pltpu.VMEM((rows, tt), F32), pltpu.VMEM((rows, tt), F32)],
        compiler_params=_cparams("parallel"),
        name="peer_route",
    )(xT, wqT, ka_kh, kb_kh, kb_hk)


def _peer_weights(w_q, keys_a, keys_b, u, v):
    half = PEER_QDIM // 2
    wqT = w_q.reshape(D_MODEL, PEER_HEADS, 2, half).transpose(2, 1, 3, 0)
    wqT = wqT.reshape(PEER_HEADS * PEER_QDIM, D_MODEL)
    eye = jnp.eye(PEER_HEADS, dtype=keys_a.dtype)
    n = PEER_HEADS * N_KEYS
    ka_kh = jnp.einsum('hkd,hg->khgd', keys_a, eye).reshape(n, PEER_HEADS * half)
    kb_kh = jnp.einsum('hkd,hg->khgd', keys_b, eye).reshape(n, PEER_HEADS * half)
    kb_hk = jnp.einsum('hkd,hg->hkgd', keys_b, eye).reshape(n, PEER_HEADS * half)
    vT = v.reshape(N_EXPERTS // PEER_UNIT, PEER_UNIT, D_MODEL).transpose(0, 2, 1)
    return tuple(t.astype(BF16) for t in (wqT, ka_kh, kb_kh, kb_hk, u, vT))


def _gelu_x2(x):
    return x * (1.0 + lax.erf(x * (2.0 ** -0.5)))


def _peer_dense_kernel(xT_ref, u_ref, vT_ref, ea_ref, ta_ref, sb_ref, eb_ref, x_ref, g_ref, b_ref,
                       o_ref, acc_ref, w_scr):
    j = pl.program_id(1)
    nu = u_ref.shape[0] // PEER_UNIT
    tt = xT_ref.shape[1]
    na = PEER_UNIT // N_KEYS

    @pl.when(j == 0)
    def _():
        acc_ref[...] = jnp.zeros_like(acc_ref)
        w_scr[2] = jnp.zeros(w_scr.shape[1:], BF16)

    for tb in range(tt // MXU_TILE):
        tok = slice(tb * MXU_TILE, (tb + 1) * MXU_TILE)

        def unit(n, carry):
            slot = n % 2
            u_rows = pl.ds(pl.multiple_of(n * PEER_UNIT, PEER_UNIT), PEER_UNIT)
            hv = jnp.dot(u_ref[u_rows, :], xT_ref[:, tok], preferred_element_type=F32)
            acc_ref[:, tok] += jnp.dot(vT_ref[jnp.maximum(n - 1, 0)],
                                       w_scr[jnp.where(n == 0, 2, 1 - slot)],
                                       preferred_element_type=F32)
            for g in range(MXU_TILE // LANES):
                lanes = slice(tb * MXU_TILE + g * LANES, tb * MXU_TILE + (g + 1) * LANES)
                cols = slice(g * LANES, (g + 1) * LANES)
                for bs in range(N_KEYS // GATE_ROWS):
                    gate = [jnp.zeros((GATE_ROWS, LANES), F32) for _ in range(na)]
                    for h in range(PEER_HEADS):
                        b_rows = slice(h * N_KEYS + bs * GATE_ROWS, h * N_KEYS + (bs + 1) * GATE_ROWS)
                        sb = sb_ref[b_rows, lanes]
                        eb = eb_ref[b_rows, lanes]
                        for al in range(na):
                            a = (j * nu + n) * na + al
                            gate[al] = gate[al] + jnp.where(sb >= ta_ref[a, h:h + 1, lanes],
                                                            eb * ea_ref[a, h:h + 1, lanes], 0.0)
                    for al in range(na):
                        rows = slice(al * N_KEYS + bs * GATE_ROWS, al * N_KEYS + (bs + 1) * GATE_ROWS)
                        w_scr[slot, rows, cols] = (gate[al] * _gelu_x2(hv[rows, cols])).astype(BF16)
            return carry

        lax.fori_loop(0, nu, unit, 0)
        acc_ref[:, tok] += jnp.dot(vT_ref[nu - 1], w_scr[(nu - 1) % 2], preferred_element_type=F32)

    @pl.when(j == pl.num_programs(1) - 1)
    def _():
        o_ref[...] = _layer_norm_rows(ALPHA * x_ref[...] + acc_ref[...].T, g_ref[...], b_ref[...])


def _peer_dense(x, xT, u, vT, ea, ta, sb, eb, g, b, tt, ec):
    D, T = xT.shape
    E = u.shape[0]
    assert T % tt == 0 and E % ec == 0 and ec % PEER_UNIT == 0 and tt % MXU_TILE == 0
    rows = PEER_HEADS * N_KEYS
    nu = ec // PEER_UNIT
    return pl.pallas_call(
        _peer_dense_kernel,
        grid=(T // tt, E // ec),
        in_specs=[pl.BlockSpec((D, tt), lambda i, j: (0, i)),
                  pl.BlockSpec((ec, D), lambda i, j: (j, 0)),
                  pl.BlockSpec((nu, D, PEER_UNIT), lambda i, j: (j, 0, 0))]
                 + [pl.BlockSpec((N_KEYS, PEER_HEADS, tt), lambda i, j: (0, 0, i))] * 2
                 + [pl.BlockSpec((rows, tt), lambda i, j: (0, i))] * 2
                 + [pl.BlockSpec((tt, D), lambda i, j: (i, 0)),
                    pl.BlockSpec((1, D), lambda i, j: (0, 0)),
                    pl.BlockSpec((1, D), lambda i, j: (0, 0))],
        out_specs=pl.BlockSpec((tt, D), lambda i, j: (i, 0)),
        out_shape=jax.ShapeDtypeStruct((T, D), F32),
        scratch_shapes=[pltpu.VMEM((D, tt), F32), pltpu.VMEM((3, PEER_UNIT, MXU_TILE), BF16)],
        compiler_params=_cparams("parallel", "arbitrary"),
        name="peer_dense",
    )(xT, u, vT, ea, ta, sb, eb, x, g.reshape(1, D), b.reshape(1, D))


def _peer_ln(x, xT, weights, g, b, tt=512, ec=2048):
    wqT, ka_kh, kb_kh, kb_hk, u, vT = weights
    tt = min(tt, x.shape[0])
    ea, ta, sb, eb = _peer_route(xT, wqT, ka_kh, kb_kh, kb_hk, tt)
    return _peer_dense(x, xT, u, vT, ea, ta, sb, eb, g, b, tt, ec)


def _rope_tables(length):
    half = RET_DK // 2
    inv = ROPE_BASE ** (-jnp.arange(half, dtype=F32) / half)
    ang = jnp.arange(length, dtype=F32)[:, None] * inv[None, :]
    return jnp.cos(ang), jnp.sin(ang)


def _heads_first(t, batch, seqlen):
    return t.reshape(batch, seqlen, SB_HEADS, SB_HD).transpose(0, 2, 1, 3)


def kernel(x_prompt, x_sample, state_ret, cache_k, cache_v, ret_w_in, ret_w_o, sb_w_qkv, sb_w_o,
           peer_w_q, peer_keys_a, peer_keys_b, peer_u, peer_v, ln_g, ln_b):
    B, L, D = x_prompt.shape
    Bs, Ls, _ = x_sample.shape
    xp = x_prompt.reshape(B * L, D)
    xs = x_sample.reshape(Bs * Ls, D)
    cos, sin = _rope_tables(max(L, PAST_LEN + Ls))

    w_in = ret_w_in[0].astype(BF16)
    w_o = ret_w_o[0].astype(BF16)
    s0_p = jnp.zeros((B, RET_HEADS, RET_DK, RET_DV), F32)
    yp, ret_p = _retention(_mm(xp, w_in, F32, tm=1024, tn=2048), s0_p, cos, sin, B, L, 0, 256)
    ys, ret_s = _retention(_mm(xs, w_in, F32), state_ret[0], cos, sin, Bs, Ls, PAST_LEN, Ls)
    xp, xpT = _mm_res_ln(yp, w_o, xp, ln_g[0, 0], ln_b[0, 0])
    xs, xsT = _mm_res_ln(ys, w_o, xs, ln_g[0, 0], ln_b[0, 0])

    def peer_layer(i, xp, xpT, xs, xsT):
        weights = _peer_weights(peer_w_q[i], peer_keys_a[i], peer_keys_b[i], peer_u[i], peer_v[i])
        xp = _peer_ln(xp, xpT, weights, ln_g[i, 1], ln_b[i, 1])
        xs = _peer_ln(xs, xsT, weights, ln_g[i, 1], ln_b[i, 1])
        return xp, xs

    xp, xs = peer_layer(0, xp, xpT, xs, xsT)

    w_qkv = sb_w_qkv[0].astype(BF16)
    w_o = sb_w_o[0].astype(BF16)
    qkv_p = _mm(xp, w_qkv, F32, tm=1024, tn=1536)
    qkv_s = _mm(xs, w_qkv, F32)
    kp, vp = [_heads_first(qkv_p[:, c * D:(c + 1) * D], B, L) for c in (1, 2)]
    ks, vs = [_heads_first(qkv_s[:, c * D:(c + 1) * D], Bs, Ls) for c in (1, 2)]
    op = _stickbreak(qkv_p, 0, qkv_p, D, qkv_p, 2 * D, B, L, L, 0, 256, 256, 256)
    lk = PAST_LEN + Ls

    def with_past(cache, new_rows):
        past = cache.transpose(0, 2, 1, 3).reshape(Bs, PAST_LEN, D)
        return jnp.concatenate([past, new_rows.reshape(Bs, Ls, D)], axis=1).reshape(Bs * lk, D)

    k_all = with_past(cache_k[0], qkv_s[:, D:2 * D])
    v_all = with_past(cache_v[0], qkv_s[:, 2 * D:])
    os_ = _stickbreak(qkv_s, 0, k_all, 0, v_all, 0, Bs, Ls, lk, PAST_LEN, Ls, Ls, 256)
    xp, xpT = _mm_res_ln(op, w_o, xp, ln_g[1, 0], ln_b[1, 0])
    xs, xsT = _mm_res_ln(os_, w_o, xs, ln_g[1, 0], ln_b[1, 0])

    xp, xs = peer_layer(1, xp, xpT, xs, xsT)

    return (xp.reshape(B, L, D), xs.reshape(Bs, Ls, D), ret_p[None], ret_s[None],
            kp[None], vp[None], ks[None], vs[None])
```

```python
import functools

import jax
import jax.numpy as jnp
from jax import lax
from jax.experimental import pallas as pl
from jax.experimental.pallas import tpu as pltpu

D_MODEL = 1024
DEPTH = 2
PAST_LEN = 4096
RET_HEADS = 4
RET_DK = D_MODEL // RET_HEADS
RET_DV = 2 * RET_DK
RET_QK = RET_HEADS * RET_DK
RET_VW = RET_HEADS * RET_DV
ROPE_BASE = 10000.0
SB_HEADS = 16
SB_HD = D_MODEL // SB_HEADS
PEER_HEADS = 8
PEER_QDIM = 256
N_KEYS = 128
N_EXPERTS = N_KEYS * N_KEYS
PEER_TOPK = 16
LN_EPS = 1e-5
GN_EPS = 1e-6
ALPHA = (2 * DEPTH) ** 0.25

LANES = 128
MXU_TILE = 256
PEER_UNIT = 2 * MXU_TILE
GATE_ROWS = 32
VMEM_LIMIT = 48 * 1024 * 1024
NEG = -1e30
LOG_F32_UNDERFLOW = 104.0

BF16 = jnp.bfloat16
F32 = jnp.float32


def _cparams(*sem):
    return pltpu.CompilerParams(dimension_semantics=sem, vmem_limit_bytes=VMEM_LIMIT)


def _mm_kernel(x_ref, w_ref, o_ref):
    o_ref[...] = jnp.dot(x_ref[...].astype(BF16), w_ref[...],
                         preferred_element_type=F32).astype(o_ref.dtype)


def _mm(x, w, out_dtype, tm=512, tn=1024):
    M, K = x.shape
    N = w.shape[1]
    tm = min(tm, M)
    tn = min(tn, N)
    assert M % tm == 0 and N % tn == 0
    return pl.pallas_call(
        _mm_kernel,
        grid=(M // tm, N // tn),
        in_specs=[pl.BlockSpec((tm, K), lambda i, j: (i, 0)),
                  pl.BlockSpec((K, tn), lambda i, j: (0, j))],
        out_specs=pl.BlockSpec((tm, tn), lambda i, j: (i, j)),
        out_shape=jax.ShapeDtypeStruct((M, N), out_dtype),
        compiler_params=_cparams("parallel", "arbitrary"),
        name="proj_matmul",
    )(x, w)


def _layer_norm_rows(r, g, b):
    mu = jnp.mean(r, axis=-1, keepdims=True)
    d = r - mu
    var = jnp.mean(d * d, axis=-1, keepdims=True)
    return d * lax.rsqrt(var + LN_EPS) * g + b


def _mm_res_ln_kernel(a_ref, w_ref, x_ref, g_ref, b_ref, o_ref, oT_ref):
    m = jnp.dot(a_ref[...].astype(BF16), w_ref[...], preferred_element_type=F32)
    y = _layer_norm_rows(ALPHA * x_ref[...] + m, g_ref[...], b_ref[...])
    o_ref[...] = y
    oT_ref[...] = y.T.astype(BF16)


def _mm_res_ln(a, w, x, g, b, tm=1024):
    M, K = a.shape
    D = w.shape[1]
    tm = min(tm, M)
    assert M % tm == 0
    return pl.pallas_call(
        _mm_res_ln_kernel,
        grid=(M // tm,),
        in_specs=[pl.BlockSpec((tm, K), lambda i: (i, 0)),
                  pl.BlockSpec((K, D), lambda i: (0, 0)),
                  pl.BlockSpec((tm, D), lambda i: (i, 0)),
                  pl.BlockSpec((1, D), lambda i: (0, 0)),
                  pl.BlockSpec((1, D), lambda i: (0, 0))],
        out_specs=[pl.BlockSpec((tm, D), lambda i: (i, 0)),
                   pl.BlockSpec((D, tm), lambda i: (0, i))],
        out_shape=[jax.ShapeDtypeStruct((M, D), F32), jax.ShapeDtypeStruct((D, M), BF16)],
        compiler_params=_cparams("parallel"),
        name="out_proj_residual_ln",
    )(a, w, x, g.reshape(1, D), b.reshape(1, D))


def _retention_kernel(q_ref, k_ref, v_ref, g_ref, cos_ref, sin_ref, dm_ref, qd_ref, kd_ref,
                      sd_ref, s0_ref, y_ref, sout_ref, s_scr):
    c = pl.program_id(2)

    @pl.when(c == 0)
    def _():
        s_scr[...] = s0_ref[0, 0]

    cos = cos_ref[...]
    sin = sin_ref[...]
    half = RET_DK // 2

    def rot(x):
        x1 = x[:, :half]
        x2 = x[:, half:]
        return jnp.concatenate([x1 * cos - x2 * sin, x1 * sin + x2 * cos], axis=-1)

    qr = rot(q_ref[...])
    kr = rot(k_ref[...]) * (RET_DK ** -0.5)
    v = v_ref[...].astype(BF16)
    S = s_scr[...]

    scores = lax.dot_general(qr.astype(BF16), kr.astype(BF16), (((1,), (1,)), ((), ())),
                             preferred_element_type=F32)
    inner = jnp.dot((scores * dm_ref[0]).astype(BF16), v, preferred_element_type=F32)
    cross = jnp.dot((qr * qd_ref[0]).astype(BF16), S.astype(BF16), preferred_element_type=F32)
    o = inner + cross

    kdec = (kr * kd_ref[0]).astype(BF16)
    s_new = S * sd_ref[0] + lax.dot_general(kdec, v, (((0,), (0,)), ((), ())),
                                            preferred_element_type=F32)
    s_scr[...] = s_new

    mu = jnp.mean(o, axis=-1, keepdims=True)
    d = o - mu
    var = jnp.mean(d * d, axis=-1, keepdims=True)
    yn = d * lax.rsqrt(var + GN_EPS)
    g = g_ref[...]
    y_ref[...] = (g * jax.nn.sigmoid(g) * yn).astype(y_ref.dtype)

    @pl.when(c == pl.num_programs(2) - 1)
    def _():
        sout_ref[0, 0] = s_new


def _retention(qkvg, s0, cos, sin, batch, seqlen, pos0, chunk):
    C = chunk
    nc = seqlen // C
    assert seqlen % C == 0 and pos0 % C == 0
    pblk = pos0 // C
    pos = jnp.arange(C, dtype=F32)
    diff = pos[:, None] - pos[None, :]
    causal = diff >= 0
    log_g = jnp.log1p(-jnp.exp2(-5.0 - jnp.arange(RET_HEADS, dtype=F32)))
    dmask = jnp.where(causal, jnp.exp(log_g[:, None, None] * jnp.where(causal, diff, 0.0)), 0.0)
    q_dec = jnp.exp(log_g[:, None] * (pos + 1.0)[None, :])
    k_dec = jnp.exp(log_g[:, None] * (C - 1.0 - pos)[None, :])
    s_dec = jnp.exp(log_g * C)
    qd = jnp.broadcast_to(q_dec[:, :, None], (RET_HEADS, C, RET_DK))
    kd = jnp.broadcast_to(k_dec[:, :, None], (RET_HEADS, C, RET_DK))
    sd = jnp.broadcast_to(s_dec[:, None, None], (RET_HEADS, 1, RET_DV))

    nq = RET_QK // RET_DK
    nv = 2 * RET_QK // RET_DV
    ng = nv + RET_VW // RET_DV
    row = lambda b, h, c: b * nc + c
    y, s_out = pl.pallas_call(
        _retention_kernel,
        grid=(batch, RET_HEADS, nc),
        in_specs=[
            pl.BlockSpec((C, RET_DK), lambda b, h, c: (row(b, h, c), h)),
            pl.BlockSpec((C, RET_DK), lambda b, h, c: (row(b, h, c), nq + h)),
            pl.BlockSpec((C, RET_DV), lambda b, h, c: (row(b, h, c), nv + h)),
            pl.BlockSpec((C, RET_DV), lambda b, h, c: (row(b, h, c), ng + h)),
            pl.BlockSpec((C, RET_DK // 2), lambda b, h, c: (pblk + c, 0)),
            pl.BlockSpec((C, RET_DK // 2), lambda b, h, c: (pblk + c, 0)),
            pl.BlockSpec((1, C, C), lambda b, h, c: (h, 0, 0)),
            pl.BlockSpec((1, C, RET_DK), lambda b, h, c: (h, 0, 0)),
            pl.BlockSpec((1, C, RET_DK), lambda b, h, c: (h, 0, 0)),
            pl.BlockSpec((1, 1, RET_DV), lambda b, h, c: (h, 0, 0)),
            pl.BlockSpec((1, 1, RET_DK, RET_DV), lambda b, h, c: (b, h, 0, 0)),
        ],
        out_specs=[
            pl.BlockSpec((C, RET_DV), lambda b, h, c: (row(b, h, c), h)),
            pl.BlockSpec((1, 1, RET_DK, RET_DV), lambda b, h, c: (b, h, 0, 0)),
        ],
        out_shape=[jax.ShapeDtypeStruct((batch * seqlen, RET_VW), BF16),
                   jax.ShapeDtypeStruct((batch, RET_HEADS, RET_DK, RET_DV), F32)],
        scratch_shapes=[pltpu.VMEM((RET_DK, RET_DV), F32)],
        compiler_params=_cparams("parallel", "parallel", "arbitrary"),
        name="retention",
    )(qkvg, qkvg, qkvg, qkvg, cos, sin, dmask, qd, kd, sd, s0)
    return y, s_out


def _log_sigmoid(z):
    return jnp.minimum(z, 0.0) - jnp.log(1.0 + jnp.exp(-jnp.abs(z)))


def _split3(x):
    hi = x.astype(BF16)
    r1 = x - hi.astype(F32)
    mid = r1.astype(BF16)
    lo = (r1 - mid.astype(F32)).astype(BF16)
    return hi, mid, lo


def _stickbreak_kernel(q_ref, k_ref, v_ref, o_ref, *, pos0, qb, kb):
    qi = pl.program_id(2)
    npair = q_ref.shape[1] // LANES
    q_first = pos0 + qi * qb
    q_pos = q_first + lax.broadcasted_iota(jnp.int32, (qb, 1), 0)
    first_head = lax.broadcasted_iota(jnp.int32, (1, LANES), 1) < SB_HD
    tri = (lax.broadcasted_iota(jnp.int32, (kb, kb), 0)
           > lax.broadcasted_iota(jnp.int32, (kb, kb), 1)).astype(BF16)
    qs = []
    for p in range(npair):
        qp = q_ref[:, p * LANES:(p + 1) * LANES] * (SB_HD ** -0.5)
        qs += [jnp.where(first_head, qp, 0.0).astype(BF16), jnp.where(first_head, 0.0, qp).astype(BF16)]
    start_blk = (q_first + qb - 2) // kb

    def cond(st):
        blk, carries, _ = st
        worst = functools.reduce(jnp.maximum, carries)
        return jnp.logical_and(blk >= 0, jnp.max(worst) > -LOG_F32_UNDERFLOW)

    def body(st):
        blk, carries, accs = st
        off = pl.multiple_of(blk * kb, kb)
        k_pos = off + lax.broadcasted_iota(jnp.int32, (1, kb), 1)
        mask = k_pos < q_pos
        heads = range(2 * npair)
        ks = [k_ref[pl.ds(off, kb), p * LANES:(p + 1) * LANES].astype(BF16) for p in range(npair)]
        vs = [v_ref[pl.ds(off, kb), p * LANES:(p + 1) * LANES].astype(BF16) for p in range(npair)]
        z = [lax.dot_general(qs[h], ks[h // 2], (((1,), (1,)), ((), ())),
                             preferred_element_type=F32) for h in heads]
        ls = [_log_sigmoid(z[h]) for h in heads]
        lr = [jnp.where(mask, ls[h] - z[h], 0.0) for h in heads]
        suffix = []
        for h in heads:
            hi, mid, lo = _split3(lr[h])
            suffix.append(jnp.dot(hi, tri, preferred_element_type=F32)
                          + jnp.dot(mid, tri, preferred_element_type=F32)
                          + jnp.dot(lo, tri, preferred_element_type=F32))
        w = [jnp.where(mask, jnp.exp(ls[h] + (carries[h] + suffix[h])), 0.0).astype(BF16)
             for h in heads]
        pv = [jnp.dot(w[h], vs[h // 2], preferred_element_type=F32) for h in heads]
        new_carries = tuple(carries[h] + jnp.sum(lr[h], axis=-1, keepdims=True) for h in heads)
        new_accs = tuple(accs[p] + jnp.where(first_head, pv[2 * p], pv[2 * p + 1])
                         for p in range(npair))
        return blk - 1, new_carries, new_accs

    init = (start_blk, tuple(jnp.zeros((qb, 1), F32) for _ in range(2 * npair)),
            tuple(jnp.zeros((qb, LANES), F32) for _ in range(npair)))
    _, _, accs = lax.while_loop(cond, body, init)
    for p in range(npair):
        o_ref[:, p * LANES:(p + 1) * LANES] = accs[p]


def _stickbreak(q_arr, q_col, k_arr, k_col, v_arr, v_col, batch, lq, lk, pos0, qb, kb, width):
    assert lq % qb == 0 and lk % kb == 0 and pos0 + lq <= lk
    assert width % LANES == 0 and D_MODEL % width == 0
    assert q_col % width == 0 and k_col % width == 0 and v_col % width == 0
    nq = lq // qb
    return pl.pallas_call(
        functools.partial(_stickbreak_kernel, pos0=pos0, qb=qb, kb=kb),
        grid=(batch, D_MODEL // width, nq),
        in_specs=[pl.BlockSpec((qb, width), lambda b, g, i: (b * nq + i, q_col // width + g)),
                  pl.BlockSpec((lk, width), lambda b, g, i: (b, k_col // width + g)),
                  pl.BlockSpec((lk, width), lambda b, g, i: (b, v_col // width + g))],
        out_specs=pl.BlockSpec((qb, width), lambda b, g, i: (b * nq + i, g)),
        out_shape=jax.ShapeDtypeStruct((batch * lq, D_MODEL), F32),
        compiler_params=_cparams("parallel", "parallel", "arbitrary"),
        name="stickbreak_attention",
    )(q_arr, k_arr, v_arr)


def _top_values(s, n):
    m = jnp.max(s, axis=0)
    vals = [m]
    for _ in range(n - 1):
        m = jnp.max(jnp.where(s < m, s, NEG), axis=0)
        vals.append(m)
    return vals


def _peer_route_kernel(xT_ref, wqT_ref, ka_kh_ref, kb_kh_ref, kb_hk_ref, ea_ref, ta_ref, sb_ref, eb_ref,
                       sa_scr, sbk_scr):
    nq = PEER_HEADS * PEER_QDIM // 2
    tt = xT_ref.shape[1]
    nsel = PEER_TOPK + 1
    q = jnp.dot(wqT_ref[...], xT_ref[...], preferred_element_type=F32).astype(BF16)
    qa = q[:nq]
    qb = q[nq:]
    sa_scr[...] = jnp.dot(ka_kh_ref[...], qa, preferred_element_type=F32)
    sb_ref[...] = jnp.dot(kb_hk_ref[...], qb, preferred_element_type=F32)
    sbk_scr[...] = jnp.dot(kb_kh_ref[...], qb, preferred_element_type=F32)
    for g in range(tt // LANES):
        lanes = slice(g * LANES, (g + 1) * LANES)
        sa3 = sa_scr[:, lanes].reshape(N_KEYS, PEER_HEADS, LANES)
        va = _top_values(sa3, nsel)
        vb = _top_values(sbk_scr[:, lanes].reshape(N_KEYS, PEER_HEADS, LANES), nsel)
        cand = jnp.stack([va[i] + vb[j] for i in range(nsel) for j in range(nsel)
                          if (i + 1) * (j + 1) <= nsel], axis=0)
        tops = _top_values(cand, nsel)
        tau = 0.5 * (tops[PEER_TOPK - 1] + tops[PEER_TOPK])
        zsum = jnp.sum(jnp.where(cand > tau, jnp.exp(cand - (va[0] + vb[0])), 0.0), axis=0)
        ea_ref[:, :, lanes] = jnp.exp(sa3 - va[0][None])
        ta_ref[:, :, lanes] = tau[None] - sa3
        zinv = 0.5 / zsum
        for h in range(PEER_HEADS):
            rows = slice(h * N_KEYS, (h + 1) * N_KEYS)
            eb_ref[rows, lanes] = jnp.exp(sb_ref[rows, lanes] - vb[0][h:h + 1]) * zinv[h:h + 1]


def _peer_route(xT, wqT, ka_kh, kb_kh, kb_hk, tt):
    D, T = xT.shape
    assert T % tt == 0
    rows = PEER_HEADS * N_KEYS
    nq = PEER_HEADS * PEER_QDIM // 2
    return pl.pallas_call(
        _peer_route_kernel,
        grid=(T // tt,),
        in_specs=[pl.BlockSpec((D, tt), lambda i: (0, i)),
                  pl.BlockSpec((2 * nq, D), lambda i: (0, 0)),
                  pl.BlockSpec((rows, nq), lambda i: (0, 0)),
                  pl.BlockSpec((rows, nq), lambda i: (0, 0)),
                  pl.BlockSpec((rows, nq), lambda i: (0, 0))],
        out_specs=[pl.BlockSpec((N_KEYS, PEER_HEADS, tt), lambda i: (0, 0, i))] * 2
                  + [pl.BlockSpec((rows, tt), lambda i: (0, i))] * 2,
        out_shape=[jax.ShapeDtypeStruct((N_KEYS, PEER_HEADS, T), F32)] * 2
                  + [jax.ShapeDtypeStruct((rows, T), F32)] * 2,
        scratch_shapes=[pltpu.VMEM((rows, tt), F32), pltpu.VMEM((rows, tt), F32)],
        compiler_params=_cparams("parallel"),
        name="peer_route",
    )(xT, wqT, ka_kh, kb_kh, kb_hk)


def _peer_weights(w_q, keys_a, keys_b, u, v):
    half = PEER_QDIM // 2
    wqT = w_q.reshape(D_MODEL, PEER_HEADS, 2, half).transpose(2, 1, 3, 0)
    wqT = wqT.reshape(PEER_HEADS * PEER_QDIM, D_MODEL)
    eye = jnp.eye(PEER_HEADS, dtype=keys_a.dtype)
    n = PEER_HEADS * N_KEYS
    ka_kh = jnp.einsum('hkd,hg->khgd', keys_a, eye).reshape(n, PEER_HEADS * half)
    kb_kh = jnp.einsum('hkd,hg->khgd', keys_b, eye).reshape(n, PEER_HEADS * half)
    kb_hk = jnp.einsum('hkd,hg->hkgd', keys_b, eye).reshape(n, PEER_HEADS * half)
    vT = v.reshape(N_EXPERTS // PEER_UNIT, PEER_UNIT, D_MODEL).transpose(0, 2, 1)
    return tuple(t.astype(BF16) for t in (wqT, ka_kh, kb_kh, kb_hk, u, vT))


def _gelu_x2(x):
    return x * (1.0 + lax.erf(x * (2.0 ** -0.5)))


def _peer_dense_kernel(xT_ref, u_ref, vT_ref, ea_ref, ta_ref, sb_ref, eb_ref, x_ref, g_ref, b_ref,
                       o_ref, acc_ref, w_scr):
    j = pl.program_id(1)
    nu = u_ref.shape[0] // PEER_UNIT
    tt = xT_ref.shape[1]
    na = PEER_UNIT // N_KEYS

    @pl.when(j == 0)
    def _():
        acc_ref[...] = jnp.zeros_like(acc_ref)

    assert nu % 2 == 0
    last_slot = (nu - 1) % 2
    for tb in range(tt // MXU_TILE):
        tok = slice(tb * MXU_TILE, (tb + 1) * MXU_TILE)
        prev_tok = slice((tb - 1) * MXU_TILE, tb * MXU_TILE)

        def unit(n, carry, tb=tb, tok=tok, prev_tok=prev_tok):
            peeled = isinstance(n, int)
            slot = n % 2
            u_rows = (slice(0, PEER_UNIT) if peeled else
                      pl.ds(pl.multiple_of(n * PEER_UNIT, PEER_UNIT), PEER_UNIT))
            hv = jnp.dot(u_ref[u_rows, :], xT_ref[:, tok], preferred_element_type=F32)
            if not peeled:
                acc_ref[:, tok] += jnp.dot(vT_ref[n - 1], w_scr[1 - slot], preferred_element_type=F32)
            elif tb > 0:
                acc_ref[:, prev_tok] += jnp.dot(vT_ref[nu - 1], w_scr[last_slot],
                                                preferred_element_type=F32)
            for g in range(MXU_TILE // LANES):
                lanes = slice(tb * MXU_TILE + g * LANES, tb * MXU_TILE + (g + 1) * LANES)
                cols = slice(g * LANES, (g + 1) * LANES)
                for bs in range(N_KEYS // GATE_ROWS):
                    gate = [jnp.zeros((GATE_ROWS, LANES), F32) for _ in range(na)]
                    for h in range(PEER_HEADS):
                        b_rows = slice(h * N_KEYS + bs * GATE_ROWS, h * N_KEYS + (bs + 1) * GATE_ROWS)
                        sb = sb_ref[b_rows, lanes]
                        eb = eb_ref[b_rows, lanes]
                        for al in range(na):
                            a = (j * nu + n) * na + al
                            gate[al] = gate[al] + jnp.where(sb >= ta_ref[a, h:h + 1, lanes],
                                                            eb * ea_ref[a, h:h + 1, lanes], 0.0)
                    for al in range(na):
                        rows = slice(al * N_KEYS + bs * GATE_ROWS, al * N_KEYS + (bs + 1) * GATE_ROWS)
                        w_scr[slot, rows, cols] = (gate[al] * _gelu_x2(hv[rows, cols])).astype(BF16)
            return carry

        unit(0, 0)
        lax.fori_loop(1, nu, unit, 0)
    acc_ref[:, tok] += jnp.dot(vT_ref[nu - 1], w_scr[last_slot], preferred_element_type=F32)

    @pl.when(j == pl.num_programs(1) - 1)
    def _():
        o_ref[...] = _layer_norm_rows(ALPHA * x_ref[...] + acc_ref[...].T, g_ref[...], b_ref[...])


def _peer_dense(x, xT, u, vT, ea, ta, sb, eb, g, b, tt, ec):
    D, T = xT.shape
    E = u.shape[0]
    assert T % tt == 0 and E % ec == 0 and ec % PEER_UNIT == 0 and tt % MXU_TILE == 0
    rows = PEER_HEADS * N_KEYS
    nu = ec // PEER_UNIT
    return pl.pallas_call(
        _peer_dense_kernel,
        grid=(T // tt, E // ec),
        in_specs=[pl.BlockSpec((D, tt), lambda i, j: (0, i)),
                  pl.BlockSpec((ec, D), lambda i, j: (j, 0)),
                  pl.BlockSpec((nu, D, PEER_UNIT), lambda i, j: (j, 0, 0))]
                 + [pl.BlockSpec((N_KEYS, PEER_HEADS, tt), lambda i, j: (0, 0, i))] * 2
                 + [pl.BlockSpec((rows, tt), lambda i, j: (0, i))] * 2
                 + [pl.BlockSpec((tt, D), lambda i, j: (i, 0)),
                    pl.BlockSpec((1, D), lambda i, j: (0, 0)),
                    pl.BlockSpec((1, D), lambda i, j: (0, 0))],
        out_specs=pl.BlockSpec((tt, D), lambda i, j: (i, 0)),
        out_shape=jax.ShapeDtypeStruct((T, D), F32),
        scratch_shapes=[pltpu.VMEM((D, tt), F32), pltpu.VMEM((2, PEER_UNIT, MXU_TILE), BF16)],
        compiler_params=_cparams("parallel", "arbitrary"),
        name="peer_dense",
    )(xT, u, vT, ea, ta, sb, eb, x, g.reshape(1, D), b.reshape(1, D))


def _peer_ln(x, xT, weights, g, b, tt=512, ec=2048):
    wqT, ka_kh, kb_kh, kb_hk, u, vT = weights
    tt = min(tt, x.shape[0])
    ea, ta, sb, eb = _peer_route(xT, wqT, ka_kh, kb_kh, kb_hk, tt)
    return _peer_dense(x, xT, u, vT, ea, ta, sb, eb, g, b, tt, ec)


def _rope_tables(length):
    half = RET_DK // 2
    inv = ROPE_BASE ** (-jnp.arange(half, dtype=F32) / half)
    ang = jnp.arange(length, dtype=F32)[:, None] * inv[None, :]
    return jnp.cos(ang), jnp.sin(ang)


def _heads_first(t, batch, seqlen):
    return t.reshape(batch, seqlen, SB_HEADS, SB_HD).transpose(0, 2, 1, 3)


def kernel(x_prompt, x_sample, state_ret, cache_k, cache_v, ret_w_in, ret_w_o, sb_w_qkv, sb_w_o,
           peer_w_q, peer_keys_a, peer_keys_b, peer_u, peer_v, ln_g, ln_b):
    B, L, D = x_prompt.shape
    Bs, Ls, _ = x_sample.shape
    xp = x_prompt.reshape(B * L, D)
    xs = x_sample.reshape(Bs * Ls, D)
    cos, sin = _rope_tables(max(L, PAST_LEN + Ls))

    w_in = ret_w_in[0].astype(BF16)
    w_o = ret_w_o[0].astype(BF16)
    s0_p = jnp.zeros((B, RET_HEADS, RET_DK, RET_DV), F32)
    yp, ret_p = _retention(_mm(xp, w_in, F32, tm=1024, tn=2048), s0_p, cos, sin, B, L, 0, 256)
    ys, ret_s = _retention(_mm(xs, w_in, F32), state_ret[0], cos, sin, Bs, Ls, PAST_LEN, Ls)
    xp, xpT = _mm_res_ln(yp, w_o, xp, ln_g[0, 0], ln_b[0, 0])
    xs, xsT = _mm_res_ln(ys, w_o, xs, ln_g[0, 0], ln_b[0, 0])

    def peer_layer(i, xp, xpT, xs, xsT):
        weights = _peer_weights(peer_w_q[i], peer_keys_a[i], peer_keys_b[i], peer_u[i], peer_v[i])
        xp = _peer_ln(xp, xpT, weights, ln_g[i, 1], ln_b[i, 1])
        xs = _peer_ln(xs, xsT, weights, ln_g[i, 1], ln_b[i, 1])
        return xp, xs

    xp, xs = peer_layer(0, xp, xpT, xs, xsT)

    w_qkv = sb_w_qkv[0].astype(BF16)
    w_o = sb_w_o[0].astype(BF16)
    qkv_p = _mm(xp, w_qkv, F32, tm=1024, tn=1536)
    qkv_s = _mm(xs, w_qkv, F32)
    kp, vp = [_heads_first(qkv_p[:, c * D:(c + 1) * D], B, L) for c in (1, 2)]
    ks, vs = [_heads_first(qkv_s[:, c * D:(c + 1) * D], Bs, Ls) for c in (1, 2)]
    op = _stickbreak(qkv_p, 0, qkv_p, D, qkv_p, 2 * D, B, L, L, 0, 256, 256, 256)
    lk = PAST_LEN + Ls

    def with_past(cache, new_rows):
        past = cache.transpose(0, 2, 1, 3).reshape(Bs, PAST_LEN, D)
        return jnp.concatenate([past, new_rows.reshape(Bs, Ls, D)], axis=1).reshape(Bs * lk, D)

    k_all = with_past(cache_k[0], qkv_s[:, D:2 * D])
    v_all = with_past(cache_v[0], qkv_s[:, 2 * D:])
    os_ = _stickbreak(qkv_s, 0, k_all, 0, v_all, 0, Bs, Ls, lk, PAST_LEN, Ls, Ls, 256)
    xp, xpT = _mm_res_ln(op, w_o, xp, ln_g[1, 0], ln_b[1, 0])
    xs, xsT = _mm_res_ln(os_, w_o, xs, ln_g[1, 0], ln_b[1, 0])

    xp, xs = peer_layer(1, xp, xpT, xs, xsT)

    return (xp.reshape(B, L, D), xs.reshape(Bs, Ls, D), ret_p[None], ret_s[None],
            kp[None], vp[None], ks[None], vs[None])
```
